```python
import math
import jax, jax.numpy as jnp
from jax import lax
import numpy as np

D_MODEL = 4096
BATCH = 1
SEQ = 8192
DEPTH = 4

D_SSD = 3 * D_MODEL // 4
SSD_HEAD_DIM = 64
SSD_HEADS = D_SSD // SSD_HEAD_DIM
SSD_GROUPS = 8
SSD_STATE = 128
SSD_CONV = 4
CHUNK = 128
SSD_GN = SSD_GROUPS * SSD_STATE
SSD_CONV_DIM = D_SSD + 2 * SSD_GN
D_SC = 3 * D_MODEL // 8
SC_GROUP_DIM = 128
SC_GROUPS = D_SC // SC_GROUP_DIM
SC_CONV = 3
N_BRANCH = 2
GATE_BLOCKS = 16
GATE_BLOCK_DIM = D_MODEL // GATE_BLOCKS
D_FF = ((8 * D_MODEL + 3 * 256 - 1) // (3 * 256)) * 256
COND_RANK = D_MODEL // 16
N_MOD = 6
EPS = 1e-6
IN_COLS = D_SSD + SSD_CONV_DIM + SSD_HEADS + 3 * D_SC

kernel_name = "hybrid_ssd_shortconv_gated_adaln_trunk"


def rms_norm(x, w):
    x32 = x.astype(jnp.float32)
    y = x32 * lax.rsqrt(jnp.mean(x32 * x32, axis=-1, keepdims=True) + EPS)
    return (y * w.astype(jnp.float32)).astype(x.dtype)


def modulate(h, shift, scale):
    return h * (1 + scale) + shift


def causal_depthwise_conv(u, w):
    k = w.shape[0]
    return lax.conv_general_dilated(
        u, w[:, None, :].astype(u.dtype), window_strides=(1,), padding=[(k - 1, 0)],
        dimension_numbers=("NWC", "WIO", "NWC"), feature_group_count=u.shape[-1])


def ssd_chunked_scan(x, dt, a, bm, cm):
    b, l, h, p = x.shape
    g, n = bm.shape[-2:]
    r = h // g
    nc = l // CHUNK
    xd = (x * dt[..., None]).reshape(b, nc, CHUNK, g, r, p)
    da = (dt * a).reshape(b, nc, CHUNK, g, r)
    bc = bm.reshape(b, nc, CHUNK, g, n)
    cc = cm.reshape(b, nc, CHUNK, g, n)
    a_cs = jnp.cumsum(da, axis=2)
    acs_t = jnp.moveaxis(a_cs, 2, -1)
    seg = acs_t[..., :, None] - acs_t[..., None, :]
    causal = jnp.tril(jnp.ones((CHUNK, CHUNK), dtype=bool))
    decay = jnp.exp(jnp.where(causal, seg, -jnp.inf))
    cb = jnp.einsum("bclgn,bcsgn->bcgls", cc, bc)
    y_diag = jnp.einsum("bcgrls,bcsgrp->bclgrp", cb[:, :, :, None] * decay, xd)
    decay_to_end = jnp.exp(a_cs[:, :, -1:] - a_cs)
    states = jnp.einsum("bcsgn,bcsgr,bcsgrp->bcgrpn", bc, decay_to_end, xd)
    chunk_decay = jnp.exp(a_cs[:, :, -1])

    def step(hs, inp):
        st, dec = inp
        return hs * dec[..., None, None] + st, hs

    h0 = jnp.zeros_like(states[:, 0])
    _, prev = lax.scan(step, h0, (jnp.moveaxis(states, 1, 0), jnp.moveaxis(chunk_decay, 1, 0)))
    prev = jnp.moveaxis(prev, 0, 1)
    y_off = jnp.einsum("bclgn,bcgrpn,bclgr->bclgrp", cc, prev, jnp.exp(a_cs))
    return (y_diag + y_off).reshape(b, l, h, p)


def hybrid_mixer(h, w_in, conv_ssd_w, conv_ssd_b, dt_bias, a_log, d_skip, ssd_norm_w,
                 sc_conv_w, w_gate, b_gate, w_br_ssd, w_br_sc, w_o):
    b, l, _ = h.shape
    proj = h @ w_in
    z, xbc, dt_raw, sc_in = jnp.split(
        proj, [D_SSD, D_SSD + SSD_CONV_DIM, D_SSD + SSD_CONV_DIM + SSD_HEADS], axis=-1)
    xbc = jax.nn.silu(causal_depthwise_conv(xbc, conv_ssd_w) + conv_ssd_b)
    xs, bm, cm = jnp.split(xbc, [D_SSD, D_SSD + SSD_GN], axis=-1)
    xs = xs.reshape(b, l, SSD_HEADS, SSD_HEAD_DIM)
    bm = bm.reshape(b, l, SSD_GROUPS, SSD_STATE)
    cm = cm.reshape(b, l, SSD_GROUPS, SSD_STATE)
    dt = jax.nn.softplus(dt_raw.astype(jnp.float32) + dt_bias.astype(jnp.float32))
    a = -jnp.exp(a_log.astype(jnp.float32))
    y = ssd_chunked_scan(xs, dt, a, bm, cm) + d_skip[:, None] * xs
    y = y.reshape(b, l, D_SSD).astype(h.dtype) * jax.nn.silu(z)
    y = rms_norm(y.reshape(b, l, SSD_GROUPS, D_SSD // SSD_GROUPS),
                 ssd_norm_w.reshape(SSD_GROUPS, D_SSD // SSD_GROUPS)).reshape(b, l, D_SSD)
    y_a = y @ w_br_ssd
    v, bg, cg = jnp.split(sc_in, 3, axis=-1)
    y_b = (bg * causal_depthwise_conv(cg * v, sc_conv_w)) @ w_br_sc
    hb = h.reshape(b, l, GATE_BLOCKS, GATE_BLOCK_DIM)
    gates = jax.nn.sigmoid(
        jnp.einsum("blkd,gkde->blgke", hb, w_gate).reshape(b, l, N_BRANCH, D_MODEL) + b_gate)
    merged = gates[:, :, 0] * y_a + gates[:, :, 1] * y_b
    return merged @ w_o


def swiglu(h, w_ffn_in, w_ffn_out):
    g, u = jnp.split(h @ w_ffn_in, 2, axis=-1)
    return (jax.nn.silu(g) * u) @ w_ffn_out


def setup_inputs(seed: int = 0) -> dict:
    key = jax.random.key(seed)
    ks = jax.random.split(key, 24)
    f32 = jnp.float32
    nrm = lambda k, shape, s: jax.random.normal(k, shape, f32) * s
    dt0 = jnp.exp(jax.random.uniform(ks[8], (DEPTH, SSD_HEADS), f32) * (math.log(0.1) - math.log(0.001))
                  + math.log(0.001))
    return {
        "x": nrm(ks[0], (BATCH, SEQ, D_MODEL), 1.0),
        "c": nrm(ks[1], (BATCH, D_MODEL), 1.0),
        "w_cond": nrm(ks[2], (D_MODEL, COND_RANK), D_MODEL ** -0.5),
        "b_cond": nrm(ks[3], (COND_RANK,), 0.02),
        "w_mod": nrm(ks[4], (DEPTH, COND_RANK, N_MOD * D_MODEL), COND_RANK ** -0.5),
        "b_mod": nrm(ks[5], (DEPTH, N_MOD * D_MODEL), 0.02),
        "norm_mix_w": 1.0 + nrm(ks[6], (DEPTH, D_MODEL), 0.02),
        "w_in": nrm(ks[7], (DEPTH, D_MODEL, IN_COLS), D_MODEL ** -0.5),
        "conv_ssd_w": nrm(ks[9], (DEPTH, SSD_CONV, SSD_CONV_DIM), SSD_CONV ** -0.5),
        "conv_ssd_b": nrm(ks[10], (DEPTH, SSD_CONV_DIM), 0.02),
        "dt_bias": dt0 + jnp.log(-jnp.expm1(-dt0)),
        "a_log": jnp.log(jax.random.uniform(ks[11], (DEPTH, SSD_HEADS), f32, 1.0, 16.0)),
        "d_skip": 1.0 + nrm(ks[12], (DEPTH, SSD_HEADS), 0.02),
        "ssd_norm_w": 1.0 + nrm(ks[13], (DEPTH, D_SSD), 0.02),
        "sc_conv_w": nrm(ks[14], (DEPTH, SC_CONV, D_SC), SC_CONV ** -0.5),
        "w_gate": nrm(ks[15], (DEPTH, N_BRANCH, GATE_BLOCKS, GATE_BLOCK_DIM, GATE_BLOCK_DIM),
                      GATE_BLOCK_DIM ** -0.5),
        "b_gate": nrm(ks[16], (DEPTH, N_BRANCH, D_MODEL), 0.02),
        "w_br_ssd": nrm(ks[17], (DEPTH, D_SSD, D_MODEL), D_SSD ** -0.5),
        "w_br_sc": nrm(ks[18], (DEPTH, D_SC, D_MODEL), D_SC ** -0.5),
        "w_o": nrm(ks[19], (DEPTH, D_MODEL, D_MODEL), D_MODEL ** -0.5),
        "norm_ffn_w": 1.0 + nrm(ks[20], (DEPTH, D_MODEL), 0.02),
        "w_ffn_in": nrm(ks[21], (DEPTH, D_MODEL, 2 * D_FF), D_MODEL ** -0.5),
        "w_ffn_out": nrm(ks[22], (DEPTH, D_FF, D_MODEL), D_FF ** -0.5),
        "final_norm_w": 1.0 + nrm(ks[23], (D_MODEL,), 0.02),
    }


def reference(x, c, w_cond, b_cond, w_mod, b_mod, norm_mix_w, w_in, conv_ssd_w, conv_ssd_b,
              dt_bias, a_log, d_skip, ssd_norm_w, sc_conv_w, w_gate, b_gate, w_br_ssd, w_br_sc,
              w_o, norm_ffn_w, w_ffn_in, w_ffn_out, final_norm_w):
    b = x.shape[0]
    t = jax.nn.silu(c) @ w_cond + b_cond
    for i in range(DEPTH):
        mod = (t @ w_mod[i] + b_mod[i]).reshape(b, N_MOD, D_MODEL)
        sh_m, sc_m, g_m, sh_f, sc_f, g_f = [mod[:, j, None, :] for j in range(N_MOD)]
        h = modulate(rms_norm(x, norm_mix_w[i]), sh_m, sc_m)
        x = x + g_m * hybrid_mixer(h, w_in[i], conv_ssd_w[i], conv_ssd_b[i], dt_bias[i], a_log[i],
                                   d_skip[i], ssd_norm_w[i], sc_conv_w[i], w_gate[i], b_gate[i],
                                   w_br_ssd[i], w_br_sc[i], w_o[i])
        h = modulate(rms_norm(x, norm_ffn_w[i]), sh_f, sc_f)
        x = x + g_f * swiglu(h, w_ffn_in[i], w_ffn_out[i])
    return rms_norm(x, final_norm_w)
```

```python
import functools

import jax
import jax.numpy as jnp
from jax import lax
from jax.experimental import pallas as pl
from jax.experimental.pallas import tpu as pltpu

SSD_GROUPS = 8
SSD_CHUNK = 128
N_MOD = 6
EPS = 1e-6
LANES = 128
HALO = 8
VMEM_LIMIT_BYTES = 56 * 1024 * 1024
FF_ALIGN = 1024

F32 = jnp.float32
BF16 = jnp.bfloat16


def _params(*sem):
    return pltpu.CompilerParams(dimension_semantics=sem, vmem_limit_bytes=VMEM_LIMIT_BYTES)


def _tile(n, pref):
    if n <= pref:
        return n
    t = pref - pref % LANES
    while t > LANES and n % t:
        t -= LANES
    assert n % t == 0, (n, pref)
    return t


def _silu(v):
    return v * jax.nn.sigmoid(v)


def _cond_kernel(c_ref, wc_ref, bc_ref, wm_ref, bm_ref, o_ref):
    c = c_ref[...]
    t_row = jnp.sum(_silu(c) * wc_ref[...], axis=0, keepdims=True) + bc_ref[...]
    r = t_row.shape[1]
    eye = lax.broadcasted_iota(jnp.int32, (r, r), 0) == lax.broadcasted_iota(jnp.int32, (r, r), 1)
    t_col = jnp.sum(jnp.where(eye, jnp.broadcast_to(t_row, (r, r)), 0.0), axis=1, keepdims=True)
    o_ref[0] = jnp.sum(t_col * wm_ref[0], axis=0, keepdims=True) + bm_ref[0]


def _cond(c, w_cond, b_cond, w_mod, b_mod):
    depth, r, nd = w_mod.shape
    d = w_cond.shape[0]
    n_mod = nd // d
    return pl.pallas_call(
        _cond_kernel,
        grid=(depth, n_mod),
        in_specs=[
            pl.BlockSpec((d, 1), lambda l, j: (0, 0)),
            pl.BlockSpec((d, r), lambda l, j: (0, 0)),
            pl.BlockSpec((1, r), lambda l, j: (0, 0)),
            pl.BlockSpec((1, r, d), lambda l, j: (l, 0, j)),
            pl.BlockSpec((1, 1, d), lambda l, j: (l * n_mod + j, 0, 0)),
        ],
        out_specs=pl.BlockSpec((1, 1, d), lambda l, j: (l * n_mod + j, 0, 0)),
        out_shape=jax.ShapeDtypeStruct((depth * n_mod, 1, d), F32),
        compiler_params=_params("arbitrary", "arbitrary"),
        name="cond",
    )(c.reshape(d, 1), w_cond, b_cond.reshape(1, r), w_mod, b_mod.reshape(depth * n_mod, 1, d))


def _norm_body(x_ref, w_ref, sc_ref, sh_ref):
    x = x_ref[...]
    y = x * lax.rsqrt(jnp.mean(x * x, axis=-1, keepdims=True) + EPS)
    return (y * w_ref[...]) * (1.0 + sc_ref[0]) + sh_ref[0]


def _norm_kernel(x_ref, w_ref, sc_ref, sh_ref, o_ref):
    o_ref[...] = _norm_body(x_ref, w_ref, sc_ref, sh_ref).astype(o_ref.dtype)


def _norm_dt_kernel(x_ref, w_ref, sc_ref, sh_ref, wdt_ref, o_ref, dt_ref):
    h = _norm_body(x_ref, w_ref, sc_ref, sh_ref).astype(o_ref.dtype)
    o_ref[...] = h
    dt_ref[...] = jnp.dot(h, wdt_ref[...], preferred_element_type=F32)


def _norm(x, w, mods, sc_idx, sh_idx, out_dtype, wdt=None):
    s, d = x.shape
    tm = _tile(s, 256)
    in_specs = [
        pl.BlockSpec((tm, d), lambda i: (i, 0)),
        pl.BlockSpec((1, d), lambda i: (0, 0)),
        pl.BlockSpec((1, 1, d), lambda i: (sc_idx, 0, 0)),
        pl.BlockSpec((1, 1, d), lambda i: (sh_idx, 0, 0)),
    ]
    args = [x, w.reshape(1, d), mods, mods]
    h_spec = pl.BlockSpec((tm, d), lambda i: (i, 0))
    h_shape = jax.ShapeDtypeStruct((s, d), out_dtype)
    if wdt is None:
        return pl.pallas_call(
            _norm_kernel, grid=(s // tm,), in_specs=in_specs, out_specs=h_spec, out_shape=h_shape,
            compiler_params=_params("arbitrary"), name="norm")(*args)
    n = wdt.shape[1]
    return pl.pallas_call(
        _norm_dt_kernel, grid=(s // tm,),
        in_specs=in_specs + [pl.BlockSpec((d, n), lambda i: (0, 0))],
        out_specs=[h_spec, pl.BlockSpec((tm, n), lambda i: (i, 0))],
        out_shape=[h_shape, jax.ShapeDtypeStruct((s, n), F32)],
        compiler_params=_params("arbitrary"), name="norm_dt")(*args, wdt)


def _mm_kernel(x_ref, w_ref, o_ref):
    o_ref[...] = jnp.dot(x_ref[...], w_ref[...], preferred_element_type=F32).astype(o_ref.dtype)


def _matmul(x, w, out_dtype, name):
    m, k = x.shape
    n = w.shape[1]
    tm, tn = _tile(m, 1024), _tile(n, 512)
    return pl.pallas_call(
        _mm_kernel,
        grid=(m // tm, n // tn),
        in_specs=[pl.BlockSpec((tm, k), lambda i, j: (i, 0)), pl.BlockSpec((k, tn), lambda i, j: (0, j))],
        out_specs=pl.BlockSpec((tm, tn), lambda i, j: (i, j)),
        out_shape=jax.ShapeDtypeStruct((m, n), out_dtype),
        compiler_params=_params("arbitrary", "arbitrary"), name=name)(x, w)


def _swiglu_kernel(x_ref, wg_ref, wu_ref, o_ref):
    x = x_ref[...]
    g = jnp.dot(x, wg_ref[...], preferred_element_type=F32)
    u = jnp.dot(x, wu_ref[...], preferred_element_type=F32)
    o_ref[...] = (_silu(g) * u).astype(o_ref.dtype)


def _swiglu_in(x, wg, wu):
    m, k = x.shape
    n = wg.shape[1]
    tm, tn = _tile(m, 1024), _tile(n, 512)
    w_spec = pl.BlockSpec((k, tn), lambda i, j: (0, j))
    return pl.pallas_call(
        _swiglu_kernel,
        grid=(m // tm, n // tn),
        in_specs=[pl.BlockSpec((tm, k), lambda i, j: (i, 0)), w_spec, w_spec],
        out_specs=pl.BlockSpec((tm, tn), lambda i, j: (i, j)),
        out_shape=jax.ShapeDtypeStruct((m, n), BF16),
        compiler_params=_params("arbitrary", "arbitrary"), name="swiglu_in")(x, wg, wu)


def _mm_res_kernel(a_ref, w_ref, r_ref, g_ref, o_ref, acc_ref):
    kk = pl.program_id(2)
    part = jnp.dot(a_ref[...], w_ref[...], preferred_element_type=F32)

    @pl.when(kk == 0)
    def _():
        acc_ref[...] = part

    @pl.when(kk > 0)
    def _():
        acc_ref[...] += part

    @pl.when(kk == pl.num_programs(2) - 1)
    def _():
        o_ref[...] = r_ref[...] + g_ref[0] * acc_ref[...]


def _matmul_residual(a, w, res, mods, g_idx, name):
    m, k = a.shape
    n = w.shape[1]
    tm, tk = _tile(m, 1024), _tile(k, 4096)
    tn = _tile(n, 1024 if tk < 4096 else 512)
    return pl.pallas_call(
        _mm_res_kernel,
        grid=(m // tm, n // tn, k // tk),
        in_specs=[
            pl.BlockSpec((tm, tk), lambda i, j, q: (i, q)),
            pl.BlockSpec((tk, tn), lambda i, j, q: (q, j)),
            pl.BlockSpec((tm, tn), lambda i, j, q: (i, j)),
            pl.BlockSpec((1, 1, tn), lambda i, j, q: (g_idx, 0, j)),
        ],
        out_specs=pl.BlockSpec((tm, tn), lambda i, j, q: (i, j)),
        out_shape=jax.ShapeDtypeStruct((m, n), F32),
        scratch_shapes=[pltpu.VMEM((tm, tn), F32)],
        compiler_params=_params("arbitrary", "arbitrary", "arbitrary"), name=name)(a, w, res, mods)


def _merge_kernel(ya_ref, yb_ref, h_ref, wa_ref, wb_ref, wg_ref, bg_ref, o_ref):
    nb, gbd = wg_ref.shape[1], wg_ref.shape[2]
    ya = ya_ref[...]
    yb = yb_ref[...]
    for k in range(nb):
        cols = slice(k * gbd, (k + 1) * gbd)
        hk = h_ref[:, cols]
        pa = jnp.dot(ya, wa_ref[:, cols], preferred_element_type=F32)
        pb = jnp.dot(yb, wb_ref[:, cols], preferred_element_type=F32)
        g0 = jax.nn.sigmoid(jnp.dot(hk, wg_ref[0, k], preferred_element_type=F32) + bg_ref[0:1, cols])
        g1 = jax.nn.sigmoid(jnp.dot(hk, wg_ref[1, k], preferred_element_type=F32) + bg_ref[1:2, cols])
        o_ref[:, cols] = (g0 * pa + g1 * pb).astype(o_ref.dtype)


def _merge(ya, yb, h, wa, wb, wg, bg):
    m, ka = ya.shape
    kb = yb.shape[1]
    n = wa.shape[1]
    gbd = wg.shape[2]
    tm, tn = _tile(m, 1024), _tile(n, 512)
    assert tn % gbd == 0
    nb = tn // gbd
    return pl.pallas_call(
        _merge_kernel,
        grid=(m // tm, n // tn),
        in_specs=[
            pl.BlockSpec((tm, ka), lambda i, j: (i, 0)),
            pl.BlockSpec((tm, kb), lambda i, j: (i, 0)),
            pl.BlockSpec((tm, tn), lambda i, j: (i, j)),
            pl.BlockSpec((ka, tn), lambda i, j: (0, j)),
            pl.BlockSpec((kb, tn), lambda i, j: (0, j)),
            pl.BlockSpec((2, nb, gbd, gbd), lambda i, j: (0, j, 0, 0)),
            pl.BlockSpec((2, tn), lambda i, j: (0, j)),
        ],
        out_specs=pl.BlockSpec((tm, tn), lambda i, j: (i, j)),
        out_shape=jax.ShapeDtypeStruct((m, n), BF16),
        compiler_params=_params("arbitrary", "arbitrary"), name="merge")(ya, yb, h, wa, wb, wg, bg)


def _split(v, parts):
    out = []
    for _ in range(parts):
        p = v.astype(BF16)
        out.append(p)
        v = v - p.astype(F32)
    return out


def _dot_split(lhs_parts, rhs):
    acc = None
    for p in lhs_parts:
        t = jnp.dot(p, rhs, preferred_element_type=F32)
        acc = t if acc is None else acc + t
    return acc


def _ssd_kernel(xbc_ref, z_ref, dtr_ref, sc_ref, cw_ref, cb_ref, dtb_ref, alog_ref, dskip_ref, nw_ref,
                scw_ref, e_ref, ya_ref, yb_ref, ext_ref, u_ref, ext2_ref, st_ref, *, groups, head_dim):
    q = xbc_ref.shape[0]
    cd = xbc_ref.shape[1]
    ds = z_ref.shape[1]
    dc = yb_ref.shape[1]
    n = (cd - ds) // (2 * groups)
    rp = ds // groups
    kc = cw_ref.shape[0]
    ks = scw_ref.shape[0]
    cblk = 512

    @pl.when(pl.program_id(0) == 0)
    def _():
        ext_ref[0:HALO, :] = jnp.zeros((HALO, cd), F32)
        ext2_ref[0:HALO, :] = jnp.zeros((HALO, dc), F32)
        st_ref[...] = jnp.zeros(st_ref.shape, F32)

    ext_ref[HALO:HALO + q, :] = xbc_ref[...].astype(F32)
    for c0 in range(0, cd, cblk):
        cols = slice(c0, c0 + cblk)
        acc = cb_ref[:, cols] + cw_ref[kc - 1:kc, cols] * ext_ref[HALO:HALO + q, cols]
        for j in range(kc - 1):
            r0 = HALO - (kc - 1) + j
            acc = acc + cw_ref[j:j + 1, cols] * ext_ref[r0:r0 + q, cols]
        u_ref[:, cols] = _silu(acc)
    ext_ref[0:HALO, :] = ext_ref[q:q + HALO, :]

    ext2_ref[HALO:HALO + q, :] = sc_ref[:, 2 * dc:3 * dc].astype(F32) * sc_ref[:, 0:dc].astype(F32)
    for c0 in range(0, dc, cblk):
        cols = slice(c0, c0 + cblk)
        acc = scw_ref[ks - 1:ks, cols] * ext2_ref[HALO:HALO + q, cols]
        for j in range(ks - 1):
            r0 = HALO - (ks - 1) + j
            acc = acc + scw_ref[j:j + 1, cols] * ext2_ref[r0:r0 + q, cols]
        yb_ref[:, cols] = (sc_ref[:, dc + c0:dc + c0 + cblk].astype(F32) * acc).astype(yb_ref.dtype)
    ext2_ref[0:HALO, :] = ext2_ref[q:q + HALO, :]

    x_dt = dtr_ref[...] + dtb_ref[...]
    dt = jnp.maximum(x_dt, 0.0) + jnp.log1p(jnp.exp(-jnp.abs(x_dt)))
    da = dt * (-jnp.exp(alog_ref[...]))
    row = lax.broadcasted_iota(jnp.int32, (q, q), 0)
    col = lax.broadcasted_iota(jnp.int32, (q, q), 1)
    causal = row >= col
    tri = causal.astype(BF16)
    acs = None
    for p in _split(da, 3):
        t = jnp.dot(tri, p, preferred_element_type=F32)
        acs = t if acs is None else acs + t
    acs_t = jnp.transpose(acs)
    e_acs = jnp.exp(acs)
    dt_end = dt * jnp.exp(acs[q - 1:q, :] - acs)

    e = e_ref[...]
    dt_x = _dot_split(_split(dt, 2), e)
    eacs_x = _dot_split(_split(e_acs, 2), e)
    dtend_x = _dot_split(_split(dt_end, 2), e)

    lane = lax.broadcasted_iota(jnp.int32, (1, LANES), 1)
    even = (lane < head_dim).astype(F32)
    odd = 1.0 - even
    r_heads = rp // head_dim

    for g in range(groups):
        gc = slice(g * rp, (g + 1) * rp)
        xs = u_ref[:, gc]
        bb = u_ref[:, ds + g * n:ds + (g + 1) * n].astype(BF16)
        cbf = u_ref[:, ds + (groups + g) * n:ds + (groups + g + 1) * n].astype(BF16)
        cb_mat = lax.dot_general(cbf, bb, (((1,), (1,)), ((), ())), preferred_element_type=F32)
        xd = xs * dt_x[:, gc]
        st = st_ref[g]
        y_off = jnp.dot(cbf, st.astype(BF16), preferred_element_type=F32) * eacs_x[:, gc]
        pieces = []
        for pr in range(rp // LANES):
            pc = slice(pr * LANES, (pr + 1) * LANES)
            xd_p = xd[:, pc]
            acc = None
            for half, msk in ((0, even), (1, odd)):
                hh = g * r_heads + pr * (LANES // head_dim) + half
                seg = acs[:, hh:hh + 1] - acs_t[hh:hh + 1, :]
                lm = jnp.exp(jnp.where(causal, seg, -jnp.inf))
                mh = (cb_mat * lm).astype(BF16)
                t = jnp.dot(mh, (xd_p * msk).astype(BF16), preferred_element_type=F32)
                acc = t if acc is None else acc + t
            pieces.append(acc)
        y_diag = jnp.concatenate(pieces, axis=1) if len(pieces) > 1 else pieces[0]
        y = y_diag + y_off + dskip_ref[:, gc] * xs
        y = y * _silu(z_ref[:, gc].astype(F32))
        yn = y * lax.rsqrt(jnp.mean(y * y, axis=-1, keepdims=True) + EPS)
        ya_ref[:, gc] = (yn * nw_ref[:, gc]).astype(ya_ref.dtype)
        xde = (xs * dtend_x[:, gc]).astype(BF16)
        contrib = lax.dot_general(bb, xde, (((0,), (0,)), ((), ())), preferred_element_type=F32)
        st_ref[g] = st * eacs_x[q - 1:q, gc] + contrib


def _ssd_call(xbc, z, dtr, sc, cw, cb, dtb, alog, dskip, nw, scw, e, head_dim):
    s, cd = xbc.shape
    ds = z.shape[1]
    dc = sc.shape[1] // 3
    q = SSD_CHUNK
    groups = SSD_GROUPS
    n = (cd - ds) // (2 * groups)
    rp = ds // groups
    assert s % q == 0 and rp % LANES == 0 and 2 * head_dim == LANES and n % LANES == 0
    assert cd % 512 == 0 and dc % 512 == 0
    row = lambda w: pl.BlockSpec((q, w), lambda c: (c, 0))
    full = lambda a: pl.BlockSpec(a.shape, lambda c: (0,) * a.ndim)
    params = [cw, cb, dtb, alog, dskip, nw, scw, e]
    return pl.pallas_call(
        functools.partial(_ssd_kernel, groups=groups, head_dim=head_dim),
        grid=(s // q,),
        in_specs=[row(cd), row(ds), row(dtr.shape[1]), row(3 * dc)] + [full(a) for a in params],
        out_specs=[row(ds), row(dc)],
        out_shape=[jax.ShapeDtypeStruct((s, ds), BF16), jax.ShapeDtypeStruct((s, dc), BF16)],
        scratch_shapes=[
            pltpu.VMEM((q + HALO, cd), F32),
            pltpu.VMEM((q, cd), F32),
            pltpu.VMEM((q + HALO, dc), F32),
            pltpu.VMEM((groups, n, rp), F32),
        ],
        compiler_params=_params("arbitrary"), name="ssd")(xbc, z, dtr, sc, *params)


def kernel(x, c, w_cond, b_cond, w_mod, b_mod, norm_mix_w, w_in, conv_ssd_w, conv_ssd_b, dt_bias, a_log,
           d_skip, ssd_norm_w, sc_conv_w, w_gate, b_gate, w_br_ssd, w_br_sc, w_o, norm_ffn_w, w_ffn_in,
           w_ffn_out, final_norm_w):
    b, s, d = x.shape
    assert b == 1, "modulation rows are shared by every token: one batch element only"
    depth = w_mod.shape[0]
    heads = dt_bias.shape[1]
    ds = ssd_norm_w.shape[1]
    cd = conv_ssd_w.shape[2]
    dc = sc_conv_w.shape[2]
    head_dim = ds // heads
    d_ff = w_ffn_out.shape[1]
    ff_pad = -(-d_ff // FF_ALIGN) * FF_ALIGN
    assert heads <= LANES

    mods = _cond(c, w_cond, b_cond, w_mod, b_mod)
    zero_row = jnp.zeros((1, 1, d), F32)
    mods = jnp.concatenate([mods, zero_row], axis=0)

    e = (jnp.arange(LANES)[:, None] == (jnp.arange(ds)[None, :] // head_dim)).astype(BF16)
    pad_h = lambda v: jnp.pad(v.astype(F32), (0, LANES - heads)).reshape(1, LANES)

    xf = x.reshape(s, d)
    for i in range(depth):
        m0 = i * N_MOD
        wi = w_in[i]
        w_z = wi[:, :ds].astype(BF16)
        w_xbc = wi[:, ds:ds + cd].astype(BF16)
        w_dt = jnp.pad(wi[:, ds + cd:ds + cd + heads], ((0, 0), (0, LANES - heads))).astype(BF16)
        w_sc = wi[:, ds + cd + heads:].astype(BF16)

        h, dtr = _norm(xf, norm_mix_w[i], mods, m0 + 1, m0 + 0, BF16, wdt=w_dt)
        zz = _matmul(h, w_z, BF16, "proj_z")
        xbc = _matmul(h, w_xbc, BF16, "proj_xbc")
        sc = _matmul(h, w_sc, BF16, "proj_sc")
        ya, yb = _ssd_call(
            xbc, zz, dtr, sc, conv_ssd_w[i], conv_ssd_b[i].reshape(1, cd), pad_h(dt_bias[i]), pad_h(a_log[i]),
            jnp.repeat(d_skip[i], head_dim).reshape(1, ds), ssd_norm_w[i].reshape(1, ds), sc_conv_w[i], e,
            head_dim)
        merged = _merge(ya, yb, h, w_br_ssd[i].astype(BF16), w_br_sc[i].astype(BF16),
                        w_gate[i].astype(BF16), b_gate[i])
        xf = _matmul_residual(merged, w_o[i].astype(BF16), xf, mods, m0 + 2, "out_proj")

        h2 = _norm(xf, norm_ffn_w[i], mods, m0 + 4, m0 + 3, BF16)
        wf = w_ffn_in[i]
        padc = ((0, 0), (0, ff_pad - d_ff))
        w_g = jnp.pad(wf[:, :d_ff], padc).astype(BF16)
        w_u = jnp.pad(wf[:, d_ff:], padc).astype(BF16)
        w_dn = jnp.pad(w_ffn_out[i], ((0, ff_pad - d_ff), (0, 0))).astype(BF16)
        act = _swiglu_in(h2, w_g, w_u)
        xf = _matmul_residual(act, w_dn, xf, mods, m0 + 5, "ffn_out")

    out = _norm(xf, final_norm_w, mods, depth * N_MOD, depth * N_MOD, x.dtype)
    return out.reshape(b, s, d)
```

```python
import functools

import jax
import jax.numpy as jnp
from jax import lax
from jax.experimental import pallas as pl
from jax.experimental.pallas import tpu as pltpu

SSD_GROUPS = 8
SSD_CHUNK = 128
N_MOD = 6
EPS = 1e-6
LANES = 128
HALO = 8
VMEM_LIMIT_BYTES = 56 * 1024 * 1024
FF_ALIGN = 1024
MXU_ROWS = 1024
MXU_COLS = 512

F32 = jnp.float32
BF16 = jnp.bfloat16


def _params(*sem):
    return pltpu.CompilerParams(dimension_semantics=sem, vmem_limit_bytes=VMEM_LIMIT_BYTES)


def _tile(n, pref):
    if n <= pref:
        return n
    t = pref - pref % LANES
    while t > LANES and n % t:
        t -= LANES
    assert n % t == 0, (n, pref)
    return t


def _silu(v):
    return v * jax.nn.sigmoid(v)


def _cond_kernel(c_ref, wc_ref, bc_ref, wm_ref, bm_ref, o_ref):
    c = c_ref[...]
    t_row = jnp.sum(_silu(c) * wc_ref[...], axis=0, keepdims=True) + bc_ref[...]
    r = t_row.shape[1]
    eye = lax.broadcasted_iota(jnp.int32, (r, r), 0) == lax.broadcasted_iota(jnp.int32, (r, r), 1)
    t_col = jnp.sum(jnp.where(eye, jnp.broadcast_to(t_row, (r, r)), 0.0), axis=1, keepdims=True)
    o_ref[0] = jnp.sum(t_col * wm_ref[0], axis=0, keepdims=True) + bm_ref[0]


def _cond(c, w_cond, b_cond, w_mod, b_mod):
    depth, r, nd = w_mod.shape
    d = w_cond.shape[0]
    n_mod = nd // d
    return pl.pallas_call(
        _cond_kernel,
        grid=(depth, n_mod),
        in_specs=[
            pl.BlockSpec((d, 1), lambda l, j: (0, 0)),
            pl.BlockSpec((d, r), lambda l, j: (0, 0)),
            pl.BlockSpec((1, r), lambda l, j: (0, 0)),
            pl.BlockSpec((1, r, d), lambda l, j: (l, 0, j)),
            pl.BlockSpec((1, 1, d), lambda l, j: (l * n_mod + j, 0, 0)),
        ],
        out_specs=pl.BlockSpec((1, 1, d), lambda l, j: (l * n_mod + j, 0, 0)),
        out_shape=jax.ShapeDtypeStruct((depth * n_mod, 1, d), F32),
        compiler_params=_params("arbitrary", "arbitrary"),
        name="cond",
    )(c.reshape(d, 1), w_cond, b_cond.reshape(1, r), w_mod, b_mod.reshape(depth * n_mod, 1, d))


def _norm_body(x_ref, w_ref, sc_ref, sh_ref):
    x = x_ref[...]
    y = x * lax.rsqrt(jnp.mean(x * x, axis=-1, keepdims=True) + EPS)
    return (y * w_ref[...]) * (1.0 + sc_ref[0]) + sh_ref[0]


def _norm_kernel(x_ref, w_ref, sc_ref, sh_ref, o_ref):
    o_ref[...] = _norm_body(x_ref, w_ref, sc_ref, sh_ref).astype(o_ref.dtype)


def _norm_dt_kernel(x_ref, w_ref, sc_ref, sh_ref, wdt_ref, o_ref, dt_ref):
    h = _norm_body(x_ref, w_ref, sc_ref, sh_ref).astype(o_ref.dtype)
    o_ref[...] = h
    dt_ref[...] = jnp.dot(h, wdt_ref[...], preferred_element_type=F32)


def _norm(x, w, mods, sc_idx, sh_idx, out_dtype, wdt=None):
    s, d = x.shape
    tm = _tile(s, 256)
    in_specs = [
        pl.BlockSpec((tm, d), lambda i: (i, 0)),
        pl.BlockSpec((1, d), lambda i: (0, 0)),
        pl.BlockSpec((1, 1, d), lambda i: (sc_idx, 0, 0)),
        pl.BlockSpec((1, 1, d), lambda i: (sh_idx, 0, 0)),
    ]
    args = [x, w.reshape(1, d), mods, mods]
    h_spec = pl.BlockSpec((tm, d), lambda i: (i, 0))
    h_shape = jax.ShapeDtypeStruct((s, d), out_dtype)
    if wdt is None:
        return pl.pallas_call(
            _norm_kernel, grid=(s // tm,), in_specs=in_specs, out_specs=h_spec, out_shape=h_shape,
            compiler_params=_params("arbitrary"), name="norm")(*args)
    n = wdt.shape[1]
    return pl.pallas_call(
        _norm_dt_kernel, grid=(s // tm,),
        in_specs=in_specs + [pl.BlockSpec((d, n), lambda i: (0, 0))],
        out_specs=[h_spec, pl.BlockSpec((tm, n), lambda i: (i, 0))],
        out_shape=[h_shape, jax.ShapeDtypeStruct((s, n), F32)],
        compiler_params=_params("arbitrary"), name="norm_dt")(*args, wdt)


def _mm_kernel(x_ref, w_ref, o_ref):
    w = w_ref[...].reshape(w_ref.shape[-2:]).astype(BF16)
    tr = min(MXU_ROWS, x_ref.shape[0])
    for r0 in range(0, x_ref.shape[0], tr):
        rows = slice(r0, r0 + tr)
        o_ref[rows, :] = jnp.dot(x_ref[rows, :], w, preferred_element_type=F32).astype(o_ref.dtype)


def _resident(shape, index_map):
    return pl.BlockSpec(shape, index_map, pipeline_mode=pl.Buffered(1))


def _matmul(x, w, out_dtype, name, layer=None, col0=0, n=None):
    m, k = x.shape
    n = w.shape[-1] if n is None else n
    tm, tn = _tile(m, 2048), _tile(n, 512)
    assert col0 % tn == 0 and tm % min(MXU_ROWS, tm) == 0
    j0 = col0 // tn
    if layer is None:
        w_spec = pl.BlockSpec((k, tn), lambda i, j: (0, j0 + j))
    else:
        w_spec = pl.BlockSpec((1, k, tn), lambda i, j: (layer, 0, j0 + j))
    return pl.pallas_call(
        _mm_kernel,
        grid=(m // tm, n // tn),
        in_specs=[_resident((tm, k), lambda i, j: (i, 0)), w_spec],
        out_specs=pl.BlockSpec((tm, tn), lambda i, j: (i, j)),
        out_shape=jax.ShapeDtypeStruct((m, n), out_dtype),
        compiler_params=_params("arbitrary", "arbitrary"), name=name)(x, w)


def _swiglu_kernel(x_ref, wg_ref, wu_ref, o_ref, *, d_ff):
    tn = o_ref.shape[1]
    wg = wg_ref[0].astype(BF16)
    wu = wu_ref[0].astype(BF16)
    col = pl.program_id(1) * tn + lax.broadcasted_iota(jnp.int32, (1, tn), 1)
    valid = col < d_ff
    tr = min(MXU_ROWS, x_ref.shape[0])
    for r0 in range(0, x_ref.shape[0], tr):
        rows = slice(r0, r0 + tr)
        x = x_ref[rows, :]
        g = jnp.dot(x, wg, preferred_element_type=F32)
        u = jnp.dot(x, wu, preferred_element_type=F32)
        o_ref[rows, :] = jnp.where(valid, _silu(g) * u, 0.0).astype(o_ref.dtype)


def _swiglu_in(x, w, layer, d_ff, ff_pad):
    m, k = x.shape
    tm, tn = _tile(m, 2048), 256
    assert d_ff % tn == 0 and ff_pad % tn == 0
    nv = d_ff // tn
    return pl.pallas_call(
        functools.partial(_swiglu_kernel, d_ff=d_ff),
        grid=(m // tm, ff_pad // tn),
        in_specs=[
            _resident((tm, k), lambda i, j: (i, 0)),
            pl.BlockSpec((1, k, tn), lambda i, j: (layer, 0, jnp.minimum(j, nv - 1))),
            pl.BlockSpec((1, k, tn), lambda i, j: (layer, 0, nv + jnp.minimum(j, nv - 1))),
        ],
        out_specs=pl.BlockSpec((tm, tn), lambda i, j: (i, j)),
        out_shape=jax.ShapeDtypeStruct((m, ff_pad), BF16),
        compiler_params=_params("arbitrary", "arbitrary"), name="swiglu_in")(x, w, w)


def _mm_res_kernel(a_ref, w_ref, r_ref, g_ref, o_ref):
    kk = pl.program_id(2)
    tc = min(MXU_COLS, o_ref.shape[1])

    def step(first):
        for c0 in range(0, o_ref.shape[1], tc):
            cols = slice(c0, c0 + tc)
            part = g_ref[0, :, cols] * jnp.dot(a_ref[...], w_ref[:, cols], preferred_element_type=F32)
            o_ref[:, cols] = (r_ref[:, cols] if first else o_ref[:, cols]) + part

    pl.when(kk == 0)(functools.partial(step, True))
    pl.when(kk > 0)(functools.partial(step, False))


def _matmul_residual(a, w, res, mods, g_idx, name):
    m, k = a.shape
    n = w.shape[1]
    tm, tn, tk = _tile(m, 1024), _tile(n, 1024), _tile(k, 4096)
    return pl.pallas_call(
        _mm_res_kernel,
        grid=(m // tm, n // tn, k // tk),
        in_specs=[
            pl.BlockSpec((tm, tk), lambda i, j, q: (i, q)),
            pl.BlockSpec((tk, tn), lambda i, j, q: (q, j)),
            pl.BlockSpec((tm, tn), lambda i, j, q: (i, j)),
            pl.BlockSpec((1, 1, tn), lambda i, j, q: (g_idx, 0, j)),
        ],
        out_specs=pl.BlockSpec((tm, tn), lambda i, j, q: (i, j)),
        out_shape=jax.ShapeDtypeStruct((m, n), F32),
        compiler_params=_params("arbitrary", "arbitrary", "arbitrary"), name=name)(a, w, res, mods)


def _mm_res_full_kernel(a_ref, w_ref, r_ref, g_ref, o_ref):
    w = w_ref[0].astype(BF16)
    tr = min(MXU_ROWS, a_ref.shape[0])
    for r0 in range(0, a_ref.shape[0], tr):
        rows = slice(r0, r0 + tr)
        o_ref[rows, :] = r_ref[rows, :] + g_ref[0] * jnp.dot(a_ref[rows, :], w, preferred_element_type=F32)


def _matmul_residual_full(a, w, layer, res, mods, g_idx, name):
    m, k = a.shape
    n = w.shape[-1]
    tm, tn = _tile(m, 2048), _tile(n, 256)
    return pl.pallas_call(
        _mm_res_full_kernel,
        grid=(m // tm, n // tn),
        in_specs=[
            _resident((tm, k), lambda i, j: (i, 0)),
            pl.BlockSpec((1, k, tn), lambda i, j: (layer, 0, j)),
            pl.BlockSpec((tm, tn), lambda i, j: (i, j)),
            pl.BlockSpec((1, 1, tn), lambda i, j: (g_idx, 0, j)),
        ],
        out_specs=pl.BlockSpec((tm, tn), lambda i, j: (i, j)),
        out_shape=jax.ShapeDtypeStruct((m, n), F32),
        compiler_params=_params("arbitrary", "arbitrary"), name=name)(a, w, res, mods)


def _merge_kernel(ya_ref, yb_ref, h_ref, wa_ref, wb_ref, wg_ref, bg_ref, o_ref):
    wa = wa_ref[0].astype(BF16)
    wb = wb_ref[0].astype(BF16)
    wg0 = wg_ref[0, 0, 0].astype(BF16)
    wg1 = wg_ref[0, 1, 0].astype(BF16)
    tr = min(MXU_ROWS, ya_ref.shape[0])
    for r0 in range(0, ya_ref.shape[0], tr):
        rows = slice(r0, r0 + tr)
        hk = h_ref[rows, :]
        pa = jnp.dot(ya_ref[rows, :], wa, preferred_element_type=F32)
        pb = jnp.dot(yb_ref[rows, :], wb, preferred_element_type=F32)
        g0 = jax.nn.sigmoid(jnp.dot(hk, wg0, preferred_element_type=F32) + bg_ref[0, 0:1, :])
        g1 = jax.nn.sigmoid(jnp.dot(hk, wg1, preferred_element_type=F32) + bg_ref[0, 1:2, :])
        o_ref[rows, :] = (g0 * pa + g1 * pb).astype(o_ref.dtype)


def _merge(ya, yb, h, wa, wb, wg, bg, layer):
    m, ka = ya.shape
    kb = yb.shape[1]
    n = wa.shape[-1]
    gbd = wg.shape[-1]
    tm, tn = _tile(m, 2048), gbd
    return pl.pallas_call(
        _merge_kernel,
        grid=(m // tm, n // tn),
        in_specs=[
            _resident((tm, ka), lambda i, j: (i, 0)),
            _resident((tm, kb), lambda i, j: (i, 0)),
            pl.BlockSpec((tm, tn), lambda i, j: (i, j)),
            pl.BlockSpec((1, ka, tn), lambda i, j: (layer, 0, j)),
            pl.BlockSpec((1, kb, tn), lambda i, j: (layer, 0, j)),
            pl.BlockSpec((1, 2, 1, gbd, gbd), lambda i, j: (layer, 0, j, 0, 0)),
            pl.BlockSpec((1, 2, tn), lambda i, j: (layer, 0, j)),
        ],
        out_specs=pl.BlockSpec((tm, tn), lambda i, j: (i, j)),
        out_shape=jax.ShapeDtypeStruct((m, n), BF16),
        compiler_params=_params("arbitrary", "arbitrary"), name="merge")(ya, yb, h, wa, wb, wg, bg)


def _split(v, parts):
    out = []
    for _ in range(parts):
        p = v.astype(BF16)
        out.append(p)
        v = v - p.astype(F32)
    return out


def _dot_split(lhs_parts, rhs):
    acc = None
    for p in lhs_parts:
        t = jnp.dot(p, rhs, preferred_element_type=F32)
        acc = t if acc is None else acc + t
    return acc


def _ssd_kernel(xbc_ref, z_ref, dtr_ref, sc_ref, cw_ref, cb_ref, dtb_ref, alog_ref, dskip_ref, nw_ref,
                scw_ref, e_ref, ya_ref, yb_ref, ext_ref, u_ref, ext2_ref, st_ref, *, groups, head_dim):
    q = xbc_ref.shape[0]
    cd = xbc_ref.shape[1]
    ds = z_ref.shape[1]
    dc = yb_ref.shape[1]
    n = (cd - ds) // (2 * groups)
    rp = ds // groups
    kc = cw_ref.shape[0]
    ks = scw_ref.shape[0]
    cblk = 512

    @pl.when(pl.program_id(0) == 0)
    def _():
        ext_ref[0:HALO, :] = jnp.zeros((HALO, cd), F32)
        ext2_ref[0:HALO, :] = jnp.zeros((HALO, dc), F32)
        st_ref[...] = jnp.zeros(st_ref.shape, F32)

    ext_ref[HALO:HALO + q, :] = xbc_ref[...].astype(F32)
    for c0 in range(0, cd, cblk):
        cols = slice(c0, c0 + cblk)
        acc = cb_ref[:, cols] + cw_ref[kc - 1:kc, cols] * ext_ref[HALO:HALO + q, cols]
        for j in range(kc - 1):
            r0 = HALO - (kc - 1) + j
            acc = acc + cw_ref[j:j + 1, cols] * ext_ref[r0:r0 + q, cols]
        u_ref[:, cols] = _silu(acc)
    ext_ref[0:HALO, :] = ext_ref[q:q + HALO, :]

    ext2_ref[HALO:HALO + q, :] = sc_ref[:, 2 * dc:3 * dc].astype(F32) * sc_ref[:, 0:dc].astype(F32)
    for c0 in range(0, dc, cblk):
        cols = slice(c0, c0 + cblk)
        acc = scw_ref[ks - 1:ks, cols] * ext2_ref[HALO:HALO + q, cols]
        for j in range(ks - 1):
            r0 = HALO - (ks - 1) + j
            acc = acc + scw_ref[j:j + 1, cols] * ext2_ref[r0:r0 + q, cols]
        yb_ref[:, cols] = (sc_ref[:, dc + c0:dc + c0 + cblk].astype(F32) * acc).astype(yb_ref.dtype)
    ext2_ref[0:HALO, :] = ext2_ref[q:q + HALO, :]

    x_dt = dtr_ref[...] + dtb_ref[...]
    dt = jnp.maximum(x_dt, 0.0) + jnp.log1p(jnp.exp(-jnp.abs(x_dt)))
    da = dt * (-jnp.exp(alog_ref[...]))
    row = lax.broadcasted_iota(jnp.int32, (q, q), 0)
    col = lax.broadcasted_iota(jnp.int32, (q, q), 1)
    causal = row >= col
    tri = causal.astype(BF16)
    acs = None
    for p in _split(da, 3):
        t = jnp.dot(tri, p, preferred_element_type=F32)
        acs = t if acs is None else acs + t
    acs_t = jnp.transpose(acs)
    e_acs = jnp.exp(acs)
    dt_end = dt * jnp.exp(acs[q - 1:q, :] - acs)

    e = e_ref[...]
    dt_x = _dot_split(_split(dt, 2), e)
    eacs_x = _dot_split(_split(e_acs, 2), e)
    dtend_x = _dot_split(_split(dt_end, 2), e)

    lane = lax.broadcasted_iota(jnp.int32, (1, LANES), 1)
    even = (lane < head_dim).astype(F32)
    odd = 1.0 - even
    r_heads = rp // head_dim

    for g in range(groups):
        gc = slice(g * rp, (g + 1) * rp)
        xs = u_ref[:, gc]
        bb = u_ref[:, ds + g * n:ds + (g + 1) * n].astype(BF16)
        cbf = u_ref[:, ds + (groups + g) * n:ds + (groups + g + 1) * n].astype(BF16)
        cb_mat = lax.dot_general(cbf, bb, (((1,), (1,)), ((), ())), preferred_element_type=F32)
        xd = xs * dt_x[:, gc]
        st = st_ref[g]
        y_off = jnp.dot(cbf, st.astype(BF16), preferred_element_type=F32) * eacs_x[:, gc]
        pieces = []
        for pr in range(rp // LANES):
            pc = slice(pr * LANES, (pr + 1) * LANES)
            xd_p = xd[:, pc]
            acc = None
            for half, msk in ((0, even), (1, odd)):
                hh = g * r_heads + pr * (LANES // head_dim) + half
                seg = acs[:, hh:hh + 1] - acs_t[hh:hh + 1, :]
                lm = jnp.exp(jnp.where(causal, seg, -jnp.inf))
                mh = (cb_mat * lm).astype(BF16)
                t = jnp.dot(mh, (xd_p * msk).astype(BF16), preferred_element_type=F32)
                acc = t if acc is None else acc + t
            pieces.append(acc)
        y_diag = jnp.concatenate(pieces, axis=1) if len(pieces) > 1 else pieces[0]
        y = y_diag + y_off + dskip_ref[:, gc] * xs
        y = y * _silu(z_ref[:, gc].astype(F32))
        yn = y * lax.rsqrt(jnp.mean(y * y, axis=-1, keepdims=True) + EPS)
        ya_ref[:, gc] = (yn * nw_ref[:, gc]).astype(ya_ref.dtype)
        xde = (xs * dtend_x[:, gc]).astype(BF16)
        contrib = lax.dot_general(bb, xde, (((0,), (0,)), ((), ())), preferred_element_type=F32)
        st_ref[g] = st * eacs_x[q - 1:q, gc] + contrib


def _ssd_call(xbc, z, dtr, sc, cw, cb, dtb, alog, dskip, nw, scw, e, head_dim):
    s, cd = xbc.shape
    ds = z.shape[1]
    dc = sc.shape[1] // 3
    q = SSD_CHUNK
    groups = SSD_GROUPS
    n = (cd - ds) // (2 * groups)
    rp = ds // groups
    assert s % q == 0 and rp % LANES == 0 and 2 * head_dim == LANES and n % LANES == 0
    assert cd % 512 == 0 and dc % 512 == 0
    row = lambda w: pl.BlockSpec((q, w), lambda c: (c, 0))
    full = lambda a: pl.BlockSpec(a.shape, lambda c: (0,) * a.ndim)
    params = [cw, cb, dtb, alog, dskip, nw, scw, e]
    return pl.pallas_call(
        functools.partial(_ssd_kernel, groups=groups, head_dim=head_dim),
        grid=(s // q,),
        in_specs=[row(cd), row(ds), row(dtr.shape[1]), row(3 * dc)] + [full(a) for a in params],
        out_specs=[row(ds), row(dc)],
        out_shape=[jax.ShapeDtypeStruct((s, ds), BF16), jax.ShapeDtypeStruct((s, dc), BF16)],
        scratch_shapes=[
            pltpu.VMEM((q + HALO, cd), F32),
            pltpu.VMEM((q, cd), F32),
            pltpu.VMEM((q + HALO, dc), F32),
            pltpu.VMEM((groups, n, rp), F32),
        ],
        compiler_params=_params("arbitrary"), name="ssd")(xbc, z, dtr, sc, *params)


def kernel(x, c, w_cond, b_cond, w_mod, b_mod, norm_mix_w, w_in, conv_ssd_w, conv_ssd_b, dt_bias, a_log,
           d_skip, ssd_norm_w, sc_conv_w, w_gate, b_gate, w_br_ssd, w_br_sc, w_o, norm_ffn_w, w_ffn_in,
           w_ffn_out, final_norm_w):
    b, s, d = x.shape
    assert b == 1, "modulation rows are shared by every token: one batch element only"
    depth = w_mod.shape[0]
    heads = dt_bias.shape[1]
    ds = ssd_norm_w.shape[1]
    cd = conv_ssd_w.shape[2]
    dc = sc_conv_w.shape[2]
    head_dim = ds // heads
    d_ff = w_ffn_out.shape[1]
    ff_pad = -(-d_ff // FF_ALIGN) * FF_ALIGN
    assert heads <= LANES

    mods = _cond(c, w_cond, b_cond, w_mod, b_mod)
    zero_row = jnp.zeros((1, 1, d), F32)
    mods = jnp.concatenate([mods, zero_row], axis=0)

    e = (jnp.arange(LANES)[:, None] == (jnp.arange(ds)[None, :] // head_dim)).astype(BF16)
    pad_h = lambda v: jnp.pad(v.astype(F32), (0, LANES - heads)).reshape(1, LANES)

    xf = x.reshape(s, d)
    for i in range(depth):
        m0 = i * N_MOD
        wi = w_in[i]
        w_dt = jnp.pad(wi[:, ds + cd:ds + cd + heads], ((0, 0), (0, LANES - heads))).astype(BF16)
        w_sc = wi[:, ds + cd + heads:].astype(BF16)

        h, dtr = _norm(xf, norm_mix_w[i], mods, m0 + 1, m0 + 0, BF16, wdt=w_dt)
        zz = _matmul(h, w_in, BF16, "proj_z", layer=i, col0=0, n=ds)
        xbc = _matmul(h, w_in, BF16, "proj_xbc", layer=i, col0=ds, n=cd)
        sc = _matmul(h, w_sc, BF16, "proj_sc")
        ya, yb = _ssd_call(
            xbc, zz, dtr, sc, conv_ssd_w[i], conv_ssd_b[i].reshape(1, cd), pad_h(dt_bias[i]), pad_h(a_log[i]),
            jnp.repeat(d_skip[i], head_dim).reshape(1, ds), ssd_norm_w[i].reshape(1, ds), sc_conv_w[i], e,
            head_dim)
        merged = _merge(ya, yb, h, w_br_ssd, w_br_sc, w_gate, b_gate, i)
        xf = _matmul_residual_full(merged, w_o, i, xf, mods, m0 + 2, "out_proj")

        h2 = _norm(xf, norm_ffn_w[i], mods, m0 + 4, m0 + 3, BF16)
        w_dn = jnp.pad(w_ffn_out[i], ((0, ff_pad - d_ff), (0, 0))).astype(BF16)
        act = _swiglu_in(h2, w_ffn_in, i, d_ff, ff_pad)
        xf = _matmul_residual(act, w_dn, xf, mods, m0 + 5, "ffn_out")

    out = _norm(xf, final_norm_w, mods, depth * N_MOD, depth * N_MOD, x.dtype)
    return out.reshape(b, s, d)
```

```python
import functools

import jax
import jax.numpy as jnp
from jax import lax
from jax.experimental import pallas as pl
from jax.experimental.pallas import tpu as pltpu

SSD_GROUPS = 8
SSD_CHUNK = 128
N_MOD = 6
EPS = 1e-6
LANES = 128
HALO = 8
VMEM_LIMIT_BYTES = 56 * 1024 * 1024
FF_ALIGN = 1024
MXU_ROWS = 1024
MXU_COLS = 512

F32 = jnp.float32
BF16 = jnp.bfloat16


def _params(*sem):
    return pltpu.CompilerParams(dimension_semantics=sem, vmem_limit_bytes=VMEM_LIMIT_BYTES)


def _tile(n, pref):
    if n <= pref:
        return n
    t = pref - pref % LANES
    while t > LANES and n % t:
        t -= LANES
    assert n % t == 0, (n, pref)
    return t


def _silu(v):
    return v * jax.nn.sigmoid(v)


def _dot_nt(a, b):
    return lax.dot_general(a, b, (((1,), (1,)), ((), ())), preferred_element_type=F32)


def _resident(shape, index_map):
    return pl.BlockSpec(shape, index_map, pipeline_mode=pl.Buffered(1))


def _cond_kernel(c_ref, wc_ref, bc_ref, wm_ref, bm_ref, o_ref):
    c = c_ref[...]
    t_row = jnp.sum(_silu(c) * wc_ref[...], axis=0, keepdims=True) + bc_ref[...]
    r = t_row.shape[1]
    eye = lax.broadcasted_iota(jnp.int32, (r, r), 0) == lax.broadcasted_iota(jnp.int32, (r, r), 1)
    t_col = jnp.sum(jnp.where(eye, jnp.broadcast_to(t_row, (r, r)), 0.0), axis=1, keepdims=True)
    o_ref[0] = jnp.sum(t_col * wm_ref[0], axis=0, keepdims=True) + bm_ref[0]


def _cond(c, w_cond, b_cond, w_mod, b_mod):
    depth, r, nd = w_mod.shape
    d = w_cond.shape[0]
    n_mod = nd // d
    return pl.pallas_call(
        _cond_kernel,
        grid=(depth, n_mod),
        in_specs=[
            pl.BlockSpec((d, 1), lambda l, j: (0, 0)),
            pl.BlockSpec((d, r), lambda l, j: (0, 0)),
            pl.BlockSpec((1, r), lambda l, j: (0, 0)),
            pl.BlockSpec((1, r, d), lambda l, j: (l, 0, j)),
            pl.BlockSpec((1, 1, d), lambda l, j: (l * n_mod + j, 0, 0)),
        ],
        out_specs=pl.BlockSpec((1, 1, d), lambda l, j: (l * n_mod + j, 0, 0)),
        out_shape=jax.ShapeDtypeStruct((depth * n_mod, 1, d), F32),
        compiler_params=_params("arbitrary", "arbitrary"),
        name="cond",
    )(c.reshape(d, 1), w_cond, b_cond.reshape(1, r), w_mod, b_mod.reshape(depth * n_mod, 1, d))


def _norm_body(x_ref, w_ref, sc_ref, sh_ref):
    x = x_ref[...]
    y = x * lax.rsqrt(jnp.mean(x * x, axis=-1, keepdims=True) + EPS)
    return (y * w_ref[...]) * (1.0 + sc_ref[0]) + sh_ref[0]


def _norm_kernel(x_ref, w_ref, sc_ref, sh_ref, o_ref):
    o_ref[...] = _norm_body(x_ref, w_ref, sc_ref, sh_ref).astype(o_ref.dtype)


def _norm_dt_kernel(x_ref, w_ref, sc_ref, sh_ref, wdt_ref, o_ref, dt_ref, *, heads):
    h = _norm_body(x_ref, w_ref, sc_ref, sh_ref).astype(o_ref.dtype)
    o_ref[...] = h
    dt = _dot_nt(h, wdt_ref[0].astype(BF16))
    lane = lax.broadcasted_iota(jnp.int32, (1, dt.shape[1]), 1)
    dt_ref[...] = jnp.where(lane < heads, dt, 0.0)


def _norm(x, w, mods, sc_idx, sh_idx, out_dtype, wt=None, layer=None, row0=None, heads=None):
    s, d = x.shape
    tm = _tile(s, 256)
    in_specs = [
        pl.BlockSpec((tm, d), lambda i: (i, 0)),
        pl.BlockSpec((1, d), lambda i: (0, 0)),
        pl.BlockSpec((1, 1, d), lambda i: (sc_idx, 0, 0)),
        pl.BlockSpec((1, 1, d), lambda i: (sh_idx, 0, 0)),
    ]
    args = [x, w.reshape(1, d), mods, mods]
    h_spec = pl.BlockSpec((tm, d), lambda i: (i, 0))
    h_shape = jax.ShapeDtypeStruct((s, d), out_dtype)
    if wt is None:
        return pl.pallas_call(
            _norm_kernel, grid=(s // tm,), in_specs=in_specs, out_specs=h_spec, out_shape=h_shape,
            compiler_params=_params("arbitrary"), name="norm")(*args)
    assert row0 + LANES <= wt.shape[1]
    wdt_spec = pl.BlockSpec((pl.Element(1), pl.Element(LANES), pl.Element(d)), lambda i: (layer, row0, 0))
    return pl.pallas_call(
        functools.partial(_norm_dt_kernel, heads=heads), grid=(s // tm,),
        in_specs=in_specs + [wdt_spec],
        out_specs=[h_spec, pl.BlockSpec((tm, LANES), lambda i: (i, 0))],
        out_shape=[h_shape, jax.ShapeDtypeStruct((s, LANES), F32)],
        compiler_params=_params("arbitrary"), name="norm_dt")(*args, wt)


def _proj_kernel(x_ref, w_ref, o_ref):
    w = w_ref[0].astype(BF16)
    tr = min(MXU_ROWS, x_ref.shape[0])
    for r0 in range(0, x_ref.shape[0], tr):
        rows = slice(r0, r0 + tr)
        o_ref[rows, :] = _dot_nt(x_ref[rows, :], w).astype(o_ref.dtype)


def _proj(x, wt, layer, segments):
    m, k = x.shape
    tm, tn = _tile(m, 2048), MXU_COLS
    assert all(rows % tn == 0 and row0 % HALO == 0 for row0, rows in segments)
    n = sum(rows for _, rows in segments)

    def row_of(j):
        r, t0 = None, 0
        for row0, rows in segments:
            here = row0 + (j - t0) * tn
            r = here if r is None else jnp.where(j >= t0, here, r)
            t0 += rows // tn
        return pl.multiple_of(r, HALO)

    return pl.pallas_call(
        _proj_kernel,
        grid=(m // tm, n // tn),
        in_specs=[
            _resident((tm, k), lambda i, j: (i, 0)),
            pl.BlockSpec((pl.Element(1), pl.Element(tn), pl.Element(k)), lambda i, j: (layer, row_of(j), 0)),
        ],
        out_specs=pl.BlockSpec((tm, tn), lambda i, j: (i, j)),
        out_shape=jax.ShapeDtypeStruct((m, n), BF16),
        compiler_params=_params("arbitrary", "arbitrary"), name="proj")(x, wt)


def _swiglu_kernel(x_ref, wg_ref, wu_ref, wd_ref, o_ref, wdo_ref, *, d_ff, valid_slabs):
    step = pl.program_id(0) * pl.num_programs(1) + pl.program_id(1)
    wdo_ref[...] = jnp.where(step < valid_slabs, wd_ref[0], 0.0).astype(wdo_ref.dtype)
    tn = o_ref.shape[1]
    wg = wg_ref[0].astype(BF16)
    wu = wu_ref[0].astype(BF16)
    col = pl.program_id(1) * tn + lax.broadcasted_iota(jnp.int32, (1, tn), 1)
    valid = col < d_ff
    tr = min(MXU_ROWS, x_ref.shape[0])
    for r0 in range(0, x_ref.shape[0], tr):
        rows = slice(r0, r0 + tr)
        x = x_ref[rows, :]
        g = jnp.dot(x, wg, preferred_element_type=F32)
        u = jnp.dot(x, wu, preferred_element_type=F32)
        o_ref[rows, :] = jnp.where(valid, _silu(g) * u, 0.0).astype(o_ref.dtype)


def _swiglu_in(x, w, w_down, layer, ff_pad):
    m, k = x.shape
    d_ff, n_out = w_down.shape[1:]
    tm, tn = _tile(m, 2048), 256
    assert d_ff % tn == 0 and ff_pad % tn == 0
    nv = d_ff // tn
    nj = ff_pad // tn
    steps = (m // tm) * nj
    slab = ff_pad // steps
    assert slab * steps == ff_pad and slab % 16 == 0 and d_ff % slab == 0
    valid_slabs = d_ff // slab
    return pl.pallas_call(
        functools.partial(_swiglu_kernel, d_ff=d_ff, valid_slabs=valid_slabs),
        grid=(m // tm, nj),
        in_specs=[
            _resident((tm, k), lambda i, j: (i, 0)),
            pl.BlockSpec((1, k, tn), lambda i, j: (layer, 0, jnp.minimum(j, nv - 1))),
            pl.BlockSpec((1, k, tn), lambda i, j: (layer, 0, nv + jnp.minimum(j, nv - 1))),
            pl.BlockSpec((1, slab, n_out), lambda i, j: (layer, jnp.minimum(i * nj + j, valid_slabs - 1), 0)),
        ],
        out_specs=[
            pl.BlockSpec((tm, tn), lambda i, j: (i, j)),
            pl.BlockSpec((slab, n_out), lambda i, j: (i * nj + j, 0)),
        ],
        out_shape=[jax.ShapeDtypeStruct((m, ff_pad), BF16), jax.ShapeDtypeStruct((ff_pad, n_out), BF16)],
        compiler_params=_params("arbitrary", "arbitrary"), name="swiglu_in")(x, w, w, w_down)


def _mm_res_kernel(a_ref, w_ref, r_ref, g_ref, o_ref):
    kk = pl.program_id(2)
    tc = min(MXU_COLS, o_ref.shape[1])

    def step(first):
        for c0 in range(0, o_ref.shape[1], tc):
            cols = slice(c0, c0 + tc)
            part = g_ref[0, :, cols] * jnp.dot(a_ref[...], w_ref[:, cols], preferred_element_type=F32)
            o_ref[:, cols] = (r_ref[:, cols] if first else o_ref[:, cols]) + part

    pl.when(kk == 0)(functools.partial(step, True))
    pl.when(kk > 0)(functools.partial(step, False))


def _matmul_residual(a, w, res, mods, g_idx, name):
    m, k = a.shape
    n = w.shape[1]
    tm, tn, tk = _tile(m, 1024), _tile(n, 1024), _tile(k, 4096)
    return pl.pallas_call(
        _mm_res_kernel,
        grid=(m // tm, n // tn, k // tk),
        in_specs=[
            pl.BlockSpec((tm, tk), lambda i, j, q: (i, q)),
            pl.BlockSpec((tk, tn), lambda i, j, q: (q, j)),
            pl.BlockSpec((tm, tn), lambda i, j, q: (i, j)),
            pl.BlockSpec((1, 1, tn), lambda i, j, q: (g_idx, 0, j)),
        ],
        out_specs=pl.BlockSpec((tm, tn), lambda i, j, q: (i, j)),
        out_shape=jax.ShapeDtypeStruct((m, n), F32),
        compiler_params=_params("arbitrary", "arbitrary", "arbitrary"), name=name)(a, w, res, mods)


def _mm_res_full_kernel(a_ref, w_ref, r_ref, g_ref, o_ref):
    w = w_ref[0].astype(BF16)
    tr = min(MXU_ROWS, a_ref.shape[0])
    for r0 in range(0, a_ref.shape[0], tr):
        rows = slice(r0, r0 + tr)
        o_ref[rows, :] = r_ref[rows, :] + g_ref[0] * jnp.dot(a_ref[rows, :], w, preferred_element_type=F32)


def _matmul_residual_full(a, w, layer, res, mods, g_idx, name):
    m, k = a.shape
    n = w.shape[-1]
    tm, tn = _tile(m, 2048), _tile(n, 256)
    return pl.pallas_call(
        _mm_res_full_kernel,
        grid=(m // tm, n // tn),
        in_specs=[
            _resident((tm, k), lambda i, j: (i, 0)),
            pl.BlockSpec((1, k, tn), lambda i, j: (layer, 0, j)),
            pl.BlockSpec((tm, tn), lambda i, j: (i, j)),
            pl.BlockSpec((1, 1, tn), lambda i, j: (g_idx, 0, j)),
        ],
        out_specs=pl.BlockSpec((tm, tn), lambda i, j: (i, j)),
        out_shape=jax.ShapeDtypeStruct((m, n), F32),
        compiler_params=_params("arbitrary", "arbitrary"), name=name)(a, w, res, mods)


def _merge_kernel(ya_ref, yb_ref, h_ref, wa_ref, wb_ref, wg_ref, bg_ref, o_ref):
    wa = wa_ref[0].astype(BF16)
    wb = wb_ref[0].astype(BF16)
    wg0 = wg_ref[0, 0, 0].astype(BF16)
    wg1 = wg_ref[0, 1, 0].astype(BF16)
    tr = min(MXU_ROWS, ya_ref.shape[0])
    for r0 in range(0, ya_ref.shape[0], tr):
        rows = slice(r0, r0 + tr)
        hk = h_ref[rows, :]
        pa = jnp.dot(ya_ref[rows, :], wa, preferred_element_type=F32)
        pb = jnp.dot(yb_ref[rows, :], wb, preferred_element_type=F32)
        g0 = jax.nn.sigmoid(jnp.dot(hk, wg0, preferred_element_type=F32) + bg_ref[0, 0:1, :])
        g1 = jax.nn.sigmoid(jnp.dot(hk, wg1, preferred_element_type=F32) + bg_ref[0, 1:2, :])
        o_ref[rows, :] = (g0 * pa + g1 * pb).astype(o_ref.dtype)


def _merge(ya, yb, h, wa, wb, wg, bg, layer):
    m, ka = ya.shape
    kb = yb.shape[1]
    n = wa.shape[-1]
    gbd = wg.shape[-1]
    tm, tn = _tile(m, 2048), gbd
    return pl.pallas_call(
        _merge_kernel,
        grid=(m // tm, n // tn),
        in_specs=[
            _resident((tm, ka), lambda i, j: (i, 0)),
            _resident((tm, kb), lambda i, j: (i, 0)),
            pl.BlockSpec((tm, tn), lambda i, j: (i, j)),
            pl.BlockSpec((1, ka, tn), lambda i, j: (layer, 0, j)),
            pl.BlockSpec((1, kb, tn), lambda i, j: (layer, 0, j)),
            pl.BlockSpec((1, 2, 1, gbd, gbd), lambda i, j: (layer, 0, j, 0, 0)),
            pl.BlockSpec((1, 2, tn), lambda i, j: (layer, 0, j)),
        ],
        out_specs=pl.BlockSpec((tm, tn), lambda i, j: (i, j)),
        out_shape=jax.ShapeDtypeStruct((m, n), BF16),
        compiler_params=_params("arbitrary", "arbitrary"), name="merge")(ya, yb, h, wa, wb, wg, bg)


def _split(v, parts):
    out = []
    for _ in range(parts):
        p = v.astype(BF16)
        out.append(p)
        v = v - p.astype(F32)
    return out


def _dot_split(lhs_parts, rhs):
    acc = None
    for p in lhs_parts:
        t = jnp.dot(p, rhs, preferred_element_type=F32)
        acc = t if acc is None else acc + t
    return acc


def _ssd_kernel(xbc_ref, z_ref, dtr_ref, sc_ref, cw_ref, cb_ref, dtb_ref, alog_ref, dskip_ref, nw_ref,
                scw_ref, e_ref, ya_ref, yb_ref, ext_ref, u_ref, ext2_ref, st_ref, *, groups, head_dim):
    q = xbc_ref.shape[0]
    cd = xbc_ref.shape[1]
    ds = z_ref.shape[1]
    dc = yb_ref.shape[1]
    n = (cd - ds) // (2 * groups)
    rp = ds // groups
    kc = cw_ref.shape[0]
    ks = scw_ref.shape[0]
    cblk = 512

    @pl.when(pl.program_id(0) == 0)
    def _():
        ext_ref[0:HALO, :] = jnp.zeros((HALO, cd), F32)
        ext2_ref[0:HALO, :] = jnp.zeros((HALO, dc), F32)
        st_ref[...] = jnp.zeros(st_ref.shape, F32)

    ext_ref[HALO:HALO + q, :] = xbc_ref[...].astype(F32)
    for c0 in range(0, cd, cblk):
        cols = slice(c0, c0 + cblk)
        acc = cb_ref[:, cols] + cw_ref[kc - 1:kc, cols] * ext_ref[HALO:HALO + q, cols]
        for j in range(kc - 1):
            r0 = HALO - (kc - 1) + j
            acc = acc + cw_ref[j:j + 1, cols] * ext_ref[r0:r0 + q, cols]
        u_ref[:, cols] = _silu(acc)
    ext_ref[0:HALO, :] = ext_ref[q:q + HALO, :]

    ext2_ref[HALO:HALO + q, :] = sc_ref[:, 2 * dc:3 * dc].astype(F32) * sc_ref[:, 0:dc].astype(F32)
    for c0 in range(0, dc, cblk):
        cols = slice(c0, c0 + cblk)
        acc = scw_ref[ks - 1:ks, cols] * ext2_ref[HALO:HALO + q, cols]
        for j in range(ks - 1):
            r0 = HALO - (ks - 1) + j
            acc = acc + scw_ref[j:j + 1, cols] * ext2_ref[r0:r0 + q, cols]
        yb_ref[:, cols] = (sc_ref[:, dc + c0:dc + c0 + cblk].astype(F32) * acc).astype(yb_ref.dtype)
    ext2_ref[0:HALO, :] = ext2_ref[q:q + HALO, :]

    x_dt = dtr_ref[...] + dtb_ref[...]
    dt = jnp.maximum(x_dt, 0.0) + jnp.log1p(jnp.exp(-jnp.abs(x_dt)))
    da = dt * (-jnp.exp(alog_ref[...]))
    row = lax.broadcasted_iota(jnp.int32, (q, q), 0)
    col = lax.broadcasted_iota(jnp.int32, (q, q), 1)
    causal = row >= col
    tri = causal.astype(BF16)
    acs = None
    for p in _split(da, 3):
        t = jnp.dot(tri, p, preferred_element_type=F32)
        acs = t if acs is None else acs + t
    acs_t = jnp.transpose(acs)
    e_acs = jnp.exp(acs)
    dt_end = dt * jnp.exp(acs[q - 1:q, :] - acs)

    e = e_ref[...]
    dt_x = _dot_split(_split(dt, 2), e)
    eacs_x = _dot_split(_split(e_acs, 2), e)
    dtend_x = _dot_split(_split(dt_end, 2), e)

    lane = lax.broadcasted_iota(jnp.int32, (1, LANES), 1)
    even = (lane < head_dim).astype(F32)
    odd = 1.0 - even
    r_heads = rp // head_dim

    for g in range(groups):
        gc = slice(g * rp, (g + 1) * rp)
        xs = u_ref[:, gc]
        bb = u_ref[:, ds + g * n:ds + (g + 1) * n].astype(BF16)
        cbf = u_ref[:, ds + (groups + g) * n:ds + (groups + g + 1) * n].astype(BF16)
        cb_mat = _dot_nt(cbf, bb)
        xd = xs * dt_x[:, gc]
        st = st_ref[g]
        y_off = jnp.dot(cbf, st.astype(BF16), preferred_element_type=F32) * eacs_x[:, gc]
        pieces = []
        for pr in range(rp // LANES):
            pc = slice(pr * LANES, (pr + 1) * LANES)
            xd_p = xd[:, pc]
            acc = None
            for half, msk in ((0, even), (1, odd)):
                hh = g * r_heads + pr * (LANES // head_dim) + half
                seg = acs[:, hh:hh + 1] - acs_t[hh:hh + 1, :]
                lm = jnp.exp(jnp.where(causal, seg, -jnp.inf))
                mh = (cb_mat * lm).astype(BF16)
                t = jnp.dot(mh, (xd_p * msk).astype(BF16), preferred_element_type=F32)
                acc = t if acc is None else acc + t
            pieces.append(acc)
        y_diag = jnp.concatenate(pieces, axis=1) if len(pieces) > 1 else pieces[0]
        y = y_diag + y_off + dskip_ref[:, gc] * xs
        y = y * _silu(z_ref[:, gc].astype(F32))
        yn = y * lax.rsqrt(jnp.mean(y * y, axis=-1, keepdims=True) + EPS)
        ya_ref[:, gc] = (yn * nw_ref[:, gc]).astype(ya_ref.dtype)
        xde = (xs * dtend_x[:, gc]).astype(BF16)
        contrib = lax.dot_general(bb, xde, (((0,), (0,)), ((), ())), preferred_element_type=F32)
        st_ref[g] = st * eacs_x[q - 1:q, gc] + contrib


def _ssd_call(proj, dtr, cw, cb, dtb, alog, dskip, nw, scw, e, ds, cd, dc, head_dim):
    s = proj.shape[0]
    q = SSD_CHUNK
    groups = SSD_GROUPS
    n = (cd - ds) // (2 * groups)
    rp = ds // groups
    assert s % q == 0 and rp % LANES == 0 and 2 * head_dim == LANES and n % LANES == 0
    assert cd % 512 == 0 and dc % 512 == 0 and ds % LANES == 0
    window = lambda c0, w: pl.BlockSpec((pl.Element(q), pl.Element(w)), lambda c: (c * q, c0))
    row = lambda w: pl.BlockSpec((q, w), lambda c: (c, 0))
    full = lambda a: pl.BlockSpec(a.shape, lambda c: (0,) * a.ndim)
    params = [cw, cb, dtb, alog, dskip, nw, scw, e]
    return pl.pallas_call(
        functools.partial(_ssd_kernel, groups=groups, head_dim=head_dim),
        grid=(s // q,),
        in_specs=[window(ds, cd), window(0, ds), row(dtr.shape[1]), window(ds + cd, 3 * dc)]
        + [full(a) for a in params],
        out_specs=[row(ds), row(dc)],
        out_shape=[jax.ShapeDtypeStruct((s, ds), BF16), jax.ShapeDtypeStruct((s, dc), BF16)],
        scratch_shapes=[
            pltpu.VMEM((q + HALO, cd), F32),
            pltpu.VMEM((q, cd), F32),
            pltpu.VMEM((q + HALO, dc), F32),
            pltpu.VMEM((groups, n, rp), F32),
        ],
        compiler_params=_params("arbitrary"), name="ssd")(proj, proj, dtr, proj, *params)


def kernel(x, c, w_cond, b_cond, w_mod, b_mod, norm_mix_w, w_in, conv_ssd_w, conv_ssd_b, dt_bias, a_log,
           d_skip, ssd_norm_w, sc_conv_w, w_gate, b_gate, w_br_ssd, w_br_sc, w_o, norm_ffn_w, w_ffn_in,
           w_ffn_out, final_norm_w):
    b, s, d = x.shape
    assert b == 1, "modulation rows are shared by every token: one batch element only"
    depth = w_mod.shape[0]
    heads = dt_bias.shape[1]
    ds = ssd_norm_w.shape[1]
    cd = conv_ssd_w.shape[2]
    dc = sc_conv_w.shape[2]
    head_dim = ds // heads
    d_ff = w_ffn_out.shape[1]
    ff_pad = -(-d_ff // FF_ALIGN) * FF_ALIGN
    assert heads <= LANES

    mods = _cond(c, w_cond, b_cond, w_mod, b_mod)
    zero_row = jnp.zeros((1, 1, d), F32)
    mods = jnp.concatenate([mods, zero_row], axis=0)

    e = (jnp.arange(LANES)[:, None] == (jnp.arange(ds)[None, :] // head_dim)).astype(BF16)
    pad_h = lambda v: jnp.pad(v.astype(F32), (0, LANES - heads)).reshape(1, LANES)

    w_in_t = jnp.swapaxes(w_in, 1, 2)
    sc0 = ds + cd + heads

    xf = x.reshape(s, d)
    for i in range(depth):
        m0 = i * N_MOD
        h, dtr = _norm(xf, norm_mix_w[i], mods, m0 + 1, m0 + 0, BF16, wt=w_in_t, layer=i, row0=ds + cd, heads=heads)
        proj = _proj(h, w_in_t, i, [(0, ds + cd), (sc0, 3 * dc)])
        ya, yb = _ssd_call(
            proj, dtr, conv_ssd_w[i], conv_ssd_b[i].reshape(1, cd), pad_h(dt_bias[i]), pad_h(a_log[i]),
            jnp.repeat(d_skip[i], head_dim).reshape(1, ds), ssd_norm_w[i].reshape(1, ds), sc_conv_w[i], e,
            ds, cd, dc, head_dim)
        merged = _merge(ya, yb, h, w_br_ssd, w_br_sc, w_gate, b_gate, i)
        xf = _matmul_residual_full(merged, w_o, i, xf, mods, m0 + 2, "out_proj")

        h2 = _norm(xf, norm_ffn_w[i], mods, m0 + 4, m0 + 3, BF16)
        act, w_dn = _swiglu_in(h2, w_ffn_in, w_ffn_out, i, ff_pad)
        xf = _matmul_residual(act, w_dn, xf, mods, m0 + 5, "ffn_out")

    out = _norm(xf, final_norm_w, mods, depth * N_MOD, depth * N_MOD, x.dtype)
    return out.reshape(b, s, d)
```

```python
import functools

import jax
import jax.numpy as jnp
from jax import lax
from jax.experimental import pallas as pl
from jax.experimental.pallas import tpu as pltpu

SSD_GROUPS = 8
SSD_CHUNK = 128
N_MOD = 6
EPS = 1e-6
LANES = 128
HALO = 8
VMEM_LIMIT_BYTES = 56 * 1024 * 1024
K_SPLIT = 3072
MXU_ROWS = 1024
MXU_COLS = 512

F32 = jnp.float32
BF16 = jnp.bfloat16


def _params(*sem):
    return pltpu.CompilerParams(dimension_semantics=sem, vmem_limit_bytes=VMEM_LIMIT_BYTES)


def _tile(n, pref):
    if n <= pref:
        return n
    t = pref - pref % LANES
    while t > LANES and n % t:
        t -= LANES
    assert n % t == 0, (n, pref)
    return t


def _silu(v):
    return v * jax.nn.sigmoid(v)


def _dot_nt(a, b):
    return lax.dot_general(a, b, (((1,), (1,)), ((), ())), preferred_element_type=F32)


def _resident(shape, index_map):
    return pl.BlockSpec(shape, index_map, pipeline_mode=pl.Buffered(1))


def _cond_kernel(c_ref, wc_ref, bc_ref, wm_ref, bm_ref, o_ref):
    c = c_ref[...]
    t_row = jnp.sum(_silu(c) * wc_ref[...], axis=0, keepdims=True) + bc_ref[...]
    r = t_row.shape[1]
    eye = lax.broadcasted_iota(jnp.int32, (r, r), 0) == lax.broadcasted_iota(jnp.int32, (r, r), 1)
    t_col = jnp.sum(jnp.where(eye, jnp.broadcast_to(t_row, (r, r)), 0.0), axis=1, keepdims=True)
    o_ref[0] = jnp.sum(t_col * wm_ref[0], axis=0, keepdims=True) + bm_ref[0]


def _cond(c, w_cond, b_cond, w_mod, b_mod):
    depth, r, nd = w_mod.shape
    d = w_cond.shape[0]
    n_mod = nd // d
    return pl.pallas_call(
        _cond_kernel,
        grid=(depth, n_mod),
        in_specs=[
            pl.BlockSpec((d, 1), lambda l, j: (0, 0)),
            pl.BlockSpec((d, r), lambda l, j: (0, 0)),
            pl.BlockSpec((1, r), lambda l, j: (0, 0)),
            pl.BlockSpec((1, r, d), lambda l, j: (l, 0, j)),
            pl.BlockSpec((1, 1, d), lambda l, j: (l * n_mod + j, 0, 0)),
        ],
        out_specs=pl.BlockSpec((1, 1, d), lambda l, j: (l * n_mod + j, 0, 0)),
        out_shape=jax.ShapeDtypeStruct((depth * n_mod, 1, d), F32),
        compiler_params=_params("arbitrary", "arbitrary"),
        name="cond",
    )(c.reshape(d, 1), w_cond, b_cond.reshape(1, r), w_mod, b_mod.reshape(depth * n_mod, 1, d))


def _norm_body(x_ref, w_ref, sc_ref, sh_ref):
    x = x_ref[...]
    y = x * lax.rsqrt(jnp.mean(x * x, axis=-1, keepdims=True) + EPS)
    return (y * w_ref[...]) * (1.0 + sc_ref[0]) + sh_ref[0]


def _norm_kernel(x_ref, w_ref, sc_ref, sh_ref, o_ref):
    o_ref[...] = _norm_body(x_ref, w_ref, sc_ref, sh_ref).astype(o_ref.dtype)


def _norm_dt_kernel(x_ref, w_ref, sc_ref, sh_ref, wdt_ref, o_ref, dt_ref, *, heads):
    h = _norm_body(x_ref, w_ref, sc_ref, sh_ref).astype(o_ref.dtype)
    o_ref[...] = h
    dt = _dot_nt(h, wdt_ref[0].astype(BF16))
    lane = lax.broadcasted_iota(jnp.int32, (1, dt.shape[1]), 1)
    dt_ref[...] = jnp.where(lane < heads, dt, 0.0)


def _norm(x, w, mods, sc_idx, sh_idx, out_dtype, wt=None, layer=None, row0=None, heads=None):
    s, d = x.shape
    tm = _tile(s, 256)
    in_specs = [
        pl.BlockSpec((tm, d), lambda i: (i, 0)),
        pl.BlockSpec((1, d), lambda i: (0, 0)),
        pl.BlockSpec((1, 1, d), lambda i: (sc_idx, 0, 0)),
        pl.BlockSpec((1, 1, d), lambda i: (sh_idx, 0, 0)),
    ]
    args = [x, w.reshape(1, d), mods, mods]
    h_spec = pl.BlockSpec((tm, d), lambda i: (i, 0))
    h_shape = jax.ShapeDtypeStruct((s, d), out_dtype)
    if wt is None:
        return pl.pallas_call(
            _norm_kernel, grid=(s // tm,), in_specs=in_specs, out_specs=h_spec, out_shape=h_shape,
            compiler_params=_params("arbitrary"), name="norm")(*args)
    assert row0 + LANES <= wt.shape[1]
    wdt_spec = pl.BlockSpec((pl.Element(1), pl.Element(LANES), pl.Element(d)), lambda i: (layer, row0, 0))
    return pl.pallas_call(
        functools.partial(_norm_dt_kernel, heads=heads), grid=(s // tm,),
        in_specs=in_specs + [wdt_spec],
        out_specs=[h_spec, pl.BlockSpec((tm, LANES), lambda i: (i, 0))],
        out_shape=[h_shape, jax.ShapeDtypeStruct((s, LANES), F32)],
        compiler_params=_params("arbitrary"), name="norm_dt")(*args, wt)


def _proj_kernel(x_ref, w_ref, o_ref):
    w = w_ref[0].astype(BF16)
    tr = min(MXU_ROWS, x_ref.shape[0])
    for r0 in range(0, x_ref.shape[0], tr):
        rows = slice(r0, r0 + tr)
        o_ref[rows, :] = _dot_nt(x_ref[rows, :], w).astype(o_ref.dtype)


def _proj(x, wt, layer, segments):
    m, k = x.shape
    tm, tn = _tile(m, 2048), MXU_COLS
    assert all(rows % tn == 0 and row0 % HALO == 0 for row0, rows in segments)
    n = sum(rows for _, rows in segments)

    def row_of(j):
        r, t0 = None, 0
        for row0, rows in segments:
            here = row0 + (j - t0) * tn
            r = here if r is None else jnp.where(j >= t0, here, r)
            t0 += rows // tn
        return pl.multiple_of(r, HALO)

    return pl.pallas_call(
        _proj_kernel,
        grid=(m // tm, n // tn),
        in_specs=[
            _resident((tm, k), lambda i, j: (i, 0)),
            pl.BlockSpec((pl.Element(1), pl.Element(tn), pl.Element(k)), lambda i, j: (layer, row_of(j), 0)),
        ],
        out_specs=pl.BlockSpec((tm, tn), lambda i, j: (i, j)),
        out_shape=jax.ShapeDtypeStruct((m, n), BF16),
        compiler_params=_params("arbitrary", "arbitrary"), name="proj")(x, wt)


def _swiglu_kernel(x_ref, wg_ref, wu_ref, wd_ref, o_ref, wdo_ref):
    wdo_ref[...] = wd_ref[0].astype(wdo_ref.dtype)
    wg = wg_ref[0].astype(BF16)
    wu = wu_ref[0].astype(BF16)
    tr = min(MXU_ROWS, x_ref.shape[0])
    for r0 in range(0, x_ref.shape[0], tr):
        rows = slice(r0, r0 + tr)
        x = x_ref[rows, :]
        g = jnp.dot(x, wg, preferred_element_type=F32)
        u = jnp.dot(x, wu, preferred_element_type=F32)
        o_ref[rows, :] = (_silu(g) * u).astype(o_ref.dtype)


def _swiglu_in(x, w, w_down, layer):
    m, k = x.shape
    d_ff, n_out = w_down.shape[1:]
    tm, tn = _tile(m, 2048), 256
    assert d_ff % tn == 0
    nj = d_ff // tn
    steps = (m // tm) * nj
    slab = d_ff // steps
    assert slab * steps == d_ff and slab % 16 == 0
    return pl.pallas_call(
        _swiglu_kernel,
        grid=(m // tm, nj),
        in_specs=[
            _resident((tm, k), lambda i, j: (i, 0)),
            pl.BlockSpec((1, k, tn), lambda i, j: (layer, 0, j)),
            pl.BlockSpec((1, k, tn), lambda i, j: (layer, 0, nj + j)),
            pl.BlockSpec((1, slab, n_out), lambda i, j: (layer, i * nj + j, 0)),
        ],
        out_specs=[
            pl.BlockSpec((tm, tn), lambda i, j: (i, j)),
            pl.BlockSpec((slab, n_out), lambda i, j: (i * nj + j, 0)),
        ],
        out_shape=[jax.ShapeDtypeStruct((m, d_ff), BF16), jax.ShapeDtypeStruct((d_ff, n_out), BF16)],
        compiler_params=_params("arbitrary", "arbitrary"), name="swiglu_in")(x, w, w, w_down)


def _mm_res_kernel(a_ref, w_ref, r_ref, g_ref, o_ref, *, n_k, k_last):
    kk = pl.program_id(2)
    tk = a_ref.shape[1]
    tc = min(MXU_COLS, o_ref.shape[1])

    def step(first, kv):
        for c0 in range(0, o_ref.shape[1], tc):
            cols = slice(c0, c0 + tc)
            part = g_ref[0, :, cols] * jnp.dot(a_ref[:, 0:kv], w_ref[0:kv, cols], preferred_element_type=F32)
            o_ref[:, cols] = (r_ref[:, cols] if first else o_ref[:, cols]) + part

    if n_k == 1:
        step(True, k_last)
    else:
        pl.when(kk == 0)(functools.partial(step, True, tk))
        if k_last == tk:
            pl.when(kk > 0)(functools.partial(step, False, tk))
        else:
            pl.when((kk > 0) & (kk < n_k - 1))(functools.partial(step, False, tk))
            pl.when(kk == n_k - 1)(functools.partial(step, False, k_last))


def _matmul_residual(a, w, res, mods, g_idx, name):
    m, k = a.shape
    n = w.shape[1]
    tm, tn = _tile(m, 1024), _tile(n, 1024)
    n_k = -(-k // K_SPLIT)
    tk = -(-k // (n_k * 2 * LANES)) * 2 * LANES
    k_last = k - (n_k - 1) * tk
    assert 0 < k_last <= tk and k_last % LANES == 0
    return pl.pallas_call(
        functools.partial(_mm_res_kernel, n_k=n_k, k_last=k_last),
        grid=(m // tm, n // tn, n_k),
        in_specs=[
            pl.BlockSpec((tm, tk), lambda i, j, q: (i, q)),
            pl.BlockSpec((tk, tn), lambda i, j, q: (q, j)),
            pl.BlockSpec((tm, tn), lambda i, j, q: (i, j)),
            pl.BlockSpec((1, 1, tn), lambda i, j, q: (g_idx, 0, j)),
        ],
        out_specs=pl.BlockSpec((tm, tn), lambda i, j, q: (i, j)),
        out_shape=jax.ShapeDtypeStruct((m, n), F32),
        compiler_params=_params("arbitrary", "arbitrary", "arbitrary"), name=name)(a, w, res, mods)


def _mm_res_full_kernel(a_ref, w_ref, r_ref, g_ref, o_ref):
    w = w_ref[0].astype(BF16)
    tr = min(MXU_ROWS, a_ref.shape[0])
    for r0 in range(0, a_ref.shape[0], tr):
        rows = slice(r0, r0 + tr)
        o_ref[rows, :] = r_ref[rows, :] + g_ref[0] * jnp.dot(a_ref[rows, :], w, preferred_element_type=F32)


def _matmul_residual_full(a, w, layer, res, mods, g_idx, name):
    m, k = a.shape
    n = w.shape[-1]
    tm, tn = _tile(m, 2048), _tile(n, 256)
    return pl.pallas_call(
        _mm_res_full_kernel,
        grid=(m // tm, n // tn),
        in_specs=[
            _resident((tm, k), lambda i, j: (i, 0)),
            pl.BlockSpec((1, k, tn), lambda i, j: (layer, 0, j)),
            pl.BlockSpec((tm, tn), lambda i, j: (i, j)),
            pl.BlockSpec((1, 1, tn), lambda i, j: (g_idx, 0, j)),
        ],
        out_specs=pl.BlockSpec((tm, tn), lambda i, j: (i, j)),
        out_shape=jax.ShapeDtypeStruct((m, n), F32),
        compiler_params=_params("arbitrary", "arbitrary"), name=name)(a, w, res, mods)


def _merge_kernel(ya_ref, yb_ref, h_ref, wa_ref, wb_ref, wg_ref, bg_ref, o_ref):
    wa = wa_ref[0].astype(BF16)
    wb = wb_ref[0].astype(BF16)
    wg0 = wg_ref[0, 0, 0].astype(BF16)
    wg1 = wg_ref[0, 1, 0].astype(BF16)
    tr = min(MXU_ROWS, ya_ref.shape[0])
    for r0 in range(0, ya_ref.shape[0], tr):
        rows = slice(r0, r0 + tr)
        hk = h_ref[rows, :]
        pa = jnp.dot(ya_ref[rows, :], wa, preferred_element_type=F32)
        pb = jnp.dot(yb_ref[rows, :], wb, preferred_element_type=F32)
        g0 = jax.nn.sigmoid(jnp.dot(hk, wg0, preferred_element_type=F32) + bg_ref[0, 0:1, :])
        g1 = jax.nn.sigmoid(jnp.dot(hk, wg1, preferred_element_type=F32) + bg_ref[0, 1:2, :])
        o_ref[rows, :] = (g0 * pa + g1 * pb).astype(o_ref.dtype)


def _merge(ya, yb, h, wa, wb, wg, bg, layer):
    m, ka = ya.shape
    kb = yb.shape[1]
    n = wa.shape[-1]
    gbd = wg.shape[-1]
    tm, tn = _tile(m, 2048), gbd
    return pl.pallas_call(
        _merge_kernel,
        grid=(m // tm, n // tn),
        in_specs=[
            _resident((tm, ka), lambda i, j: (i, 0)),
            _resident((tm, kb), lambda i, j: (i, 0)),
            pl.BlockSpec((tm, tn), lambda i, j: (i, j)),
            pl.BlockSpec((1, ka, tn), lambda i, j: (layer, 0, j)),
            pl.BlockSpec((1, kb, tn), lambda i, j: (layer, 0, j)),
            pl.BlockSpec((1, 2, 1, gbd, gbd), lambda i, j: (layer, 0, j, 0, 0)),
            pl.BlockSpec((1, 2, tn), lambda i, j: (layer, 0, j)),
        ],
        out_specs=pl.BlockSpec((tm, tn), lambda i, j: (i, j)),
        out_shape=jax.ShapeDtypeStruct((m, n), BF16),
        compiler_params=_params("arbitrary", "arbitrary"), name="merge")(ya, yb, h, wa, wb, wg, bg)


def _split(v, parts):
    out = []
    for _ in range(parts):
        p = v.astype(BF16)
        out.append(p)
        v = v - p.astype(F32)
    return out


def _dot_split(lhs_parts, rhs):
    acc = None
    for p in lhs_parts:
        t = jnp.dot(p, rhs, preferred_element_type=F32)
        acc = t if acc is None else acc + t
    return acc


def _ssd_kernel(xbc_ref, z_ref, dtr_ref, sc_ref, cw_ref, cb_ref, dtb_ref, alog_ref, dskip_ref, nw_ref,
                scw_ref, e_ref, ya_ref, yb_ref, ext_ref, u_ref, ext2_ref, st_ref, *, groups, head_dim):
    q = xbc_ref.shape[0]
    cd = xbc_ref.shape[1]
    ds = z_ref.shape[1]
    dc = yb_ref.shape[1]
    n = (cd - ds) // (2 * groups)
    rp = ds // groups
    kc = cw_ref.shape[0]
    ks = scw_ref.shape[0]
    cblk = 512

    @pl.when(pl.program_id(0) == 0)
    def _():
        ext_ref[0:HALO, :] = jnp.zeros((HALO, cd), F32)
        ext2_ref[0:HALO, :] = jnp.zeros((HALO, dc), F32)
        st_ref[...] = jnp.zeros(st_ref.shape, F32)

    ext_ref[HALO:HALO + q, :] = xbc_ref[...].astype(F32)
    for c0 in range(0, cd, cblk):
        cols = slice(c0, c0 + cblk)
        acc = cb_ref[:, cols] + cw_ref[kc - 1:kc, cols] * ext_ref[HALO:HALO + q, cols]
        for j in range(kc - 1):
            r0 = HALO - (kc - 1) + j
            acc = acc + cw_ref[j:j + 1, cols] * ext_ref[r0:r0 + q, cols]
        u_ref[:, cols] = _silu(acc)
    ext_ref[0:HALO, :] = ext_ref[q:q + HALO, :]

    ext2_ref[HALO:HALO + q, :] = sc_ref[:, 2 * dc:3 * dc].astype(F32) * sc_ref[:, 0:dc].astype(F32)
    for c0 in range(0, dc, cblk):
        cols = slice(c0, c0 + cblk)
        acc = scw_ref[ks - 1:ks, cols] * ext2_ref[HALO:HALO + q, cols]
        for j in range(ks - 1):
            r0 = HALO - (ks - 1) + j
            acc = acc + scw_ref[j:j + 1, cols] * ext2_ref[r0:r0 + q, cols]
        yb_ref[:, cols] = (sc_ref[:, dc + c0:dc + c0 + cblk].astype(F32) * acc).astype(yb_ref.dtype)
    ext2_ref[0:HALO, :] = ext2_ref[q:q + HALO, :]

    x_dt = dtr_ref[...] + dtb_ref[...]
    dt = jnp.maximum(x_dt, 0.0) + jnp.log1p(jnp.exp(-jnp.abs(x_dt)))
    da = dt * (-jnp.exp(alog_ref[...]))
    row = lax.broadcasted_iota(jnp.int32, (q, q), 0)
    col = lax.broadcasted_iota(jnp.int32, (q, q), 1)
    causal = row >= col
    tri = causal.astype(BF16)
    acs = None
    for p in _split(da, 3):
        t = jnp.dot(tri, p, preferred_element_type=F32)
        acs = t if acs is None else acs + t
    acs_t = jnp.transpose(acs)
    e_acs = jnp.exp(acs)
    dt_end = dt * jnp.exp(acs[q - 1:q, :] - acs)

    e = e_ref[...]
    dt_x = _dot_split(_split(dt, 2), e)
    eacs_x = _dot_split(_split(e_acs, 2), e)
    dtend_x = _dot_split(_split(dt_end, 2), e)

    lane = lax.broadcasted_iota(jnp.int32, (1, LANES), 1)
    even = (lane < head_dim).astype(F32)
    odd = 1.0 - even
    r_heads = rp // head_dim

    for g in range(groups):
        gc = slice(g * rp, (g + 1) * rp)
        xs = u_ref[:, gc]
        bb = u_ref[:, ds + g * n:ds + (g + 1) * n].astype(BF16)
        cbf = u_ref[:, ds + (groups + g) * n:ds + (groups + g + 1) * n].astype(BF16)
        cb_mat = _dot_nt(cbf, bb)
        xd = xs * dt_x[:, gc]
        st = st_ref[g]
        y_off = jnp.dot(cbf, st.astype(BF16), preferred_element_type=F32) * eacs_x[:, gc]
        pieces = []
        for pr in range(rp // LANES):
            pc = slice(pr * LANES, (pr + 1) * LANES)
            xd_p = xd[:, pc]
            lhs, rhs = [], []
            for half, msk in ((0, even), (1, odd)):
                hh = g * r_heads + pr * (LANES // head_dim) + half
                seg = acs[:, hh:hh + 1] - acs_t[hh:hh + 1, :]
                lm = jnp.exp(jnp.where(causal, seg, -jnp.inf))
                lhs.append((cb_mat * lm).astype(BF16))
                rhs.append((xd_p * msk).astype(BF16))
            pieces.append(jnp.dot(jnp.concatenate(lhs, axis=1), jnp.concatenate(rhs, axis=0),
                                  preferred_element_type=F32))
        y_diag = jnp.concatenate(pieces, axis=1) if len(pieces) > 1 else pieces[0]
        y = y_diag + y_off + dskip_ref[:, gc] * xs
        y = y * _silu(z_ref[:, gc].astype(F32))
        yn = y * lax.rsqrt(jnp.mean(y * y, axis=-1, keepdims=True) + EPS)
        ya_ref[:, gc] = (yn * nw_ref[:, gc]).astype(ya_ref.dtype)
        xde = (xs * dtend_x[:, gc]).astype(BF16)
        contrib = lax.dot_general(bb, xde, (((0,), (0,)), ((), ())), preferred_element_type=F32)
        st_ref[g] = st * eacs_x[q - 1:q, gc] + contrib


def _ssd_call(proj, dtr, cw, cb, dtb, alog, dskip, nw, scw, e, ds, cd, dc, head_dim):
    s = proj.shape[0]
    q = SSD_CHUNK
    groups = SSD_GROUPS
    n = (cd - ds) // (2 * groups)
    rp = ds // groups
    assert s % q == 0 and rp % LANES == 0 and 2 * head_dim == LANES and n % LANES == 0
    assert cd % 512 == 0 and dc % 512 == 0 and ds % LANES == 0
    window = lambda c0, w: pl.BlockSpec((pl.Element(q), pl.Element(w)), lambda c: (c * q, c0))
    row = lambda w: pl.BlockSpec((q, w), lambda c: (c, 0))
    full = lambda a: pl.BlockSpec(a.shape, lambda c: (0,) * a.ndim)
    params = [cw, cb, dtb, alog, dskip, nw, scw, e]
    return pl.pallas_call(
        functools.partial(_ssd_kernel, groups=groups, head_dim=head_dim),
        grid=(s // q,),
        in_specs=[window(ds, cd), window(0, ds), row(dtr.shape[1]), window(ds + cd, 3 * dc)]
        + [full(a) for a in params],
        out_specs=[row(ds), row(dc)],
        out_shape=[jax.ShapeDtypeStruct((s, ds), BF16), jax.ShapeDtypeStruct((s, dc), BF16)],
        scratch_shapes=[
            pltpu.VMEM((q + HALO, cd), F32),
            pltpu.VMEM((q, cd), F32),
            pltpu.VMEM((q + HALO, dc), F32),
            pltpu.VMEM((groups, n, rp), F32),
        ],
        compiler_params=_params("arbitrary"), name="ssd")(proj, proj, dtr, proj, *params)


def kernel(x, c, w_cond, b_cond, w_mod, b_mod, norm_mix_w, w_in, conv_ssd_w, conv_ssd_b, dt_bias, a_log,
           d_skip, ssd_norm_w, sc_conv_w, w_gate, b_gate, w_br_ssd, w_br_sc, w_o, norm_ffn_w, w_ffn_in,
           w_ffn_out, final_norm_w):
    b, s, d = x.shape
    assert b == 1, "modulation rows are shared by every token: one batch element only"
    depth = w_mod.shape[0]
    heads = dt_bias.shape[1]
    ds = ssd_norm_w.shape[1]
    cd = conv_ssd_w.shape[2]
    dc = sc_conv_w.shape[2]
    head_dim = ds // heads
    assert heads <= LANES

    mods = _cond(c, w_cond, b_cond, w_mod, b_mod)
    zero_row = jnp.zeros((1, 1, d), F32)
    mods = jnp.concatenate([mods, zero_row], axis=0)

    e = (jnp.arange(LANES)[:, None] == (jnp.arange(ds)[None, :] // head_dim)).astype(BF16)
    pad_h = lambda v: jnp.pad(v.astype(F32), (0, LANES - heads)).reshape(1, LANES)

    w_in_t = jnp.swapaxes(w_in, 1, 2)
    sc0 = ds + cd + heads

    xf = x.reshape(s, d)
    for i in range(depth):
        m0 = i * N_MOD
        h, dtr = _norm(xf, norm_mix_w[i], mods, m0 + 1, m0 + 0, BF16, wt=w_in_t, layer=i, row0=ds + cd, heads=heads)
        proj = _proj(h, w_in_t, i, [(0, ds + cd), (sc0, 3 * dc)])
        ya, yb = _ssd_call(
            proj, dtr, conv_ssd_w[i], conv_ssd_b[i].reshape(1, cd), pad_h(dt_bias[i]), pad_h(a_log[i]),
            jnp.repeat(d_skip[i], head_dim).reshape(1, ds), ssd_norm_w[i].reshape(1, ds), sc_conv_w[i], e,
            ds, cd, dc, head_dim)
        merged = _merge(ya, yb, h, w_br_ssd, w_br_sc, w_gate, b_gate, i)
        xf = _matmul_residual_full(merged, w_o, i, xf, mods, m0 + 2, "out_proj")

        h2 = _norm(xf, norm_ffn_w[i], mods, m0 + 4, m0 + 3, BF16)
        act, w_dn = _swiglu_in(h2, w_ffn_in, w_ffn_out, i)
        xf = _matmul_residual(act, w_dn, xf, mods, m0 + 5, "ffn_out")

    out = _norm(xf, final_norm_w, mods, depth * N_MOD, depth * N_MOD, x.dtype)
    return out.reshape(b, s, d)
```

```python
import functools

import jax
import jax.numpy as jnp
from jax import lax
from jax.experimental import pallas as pl
from jax.experimental.pallas import tpu as pltpu

SSD_GROUPS = 8
SSD_CHUNK = 128
N_MOD = 6
EPS = 1e-6
LANES = 128
HALO = 8
VMEM_LIMIT_BYTES = 61 * 1024 * 1024
K_SPLIT = 3072
MXU_ROWS = 1024
MXU_COLS = 512

F32 = jnp.float32
BF16 = jnp.bfloat16


def _params(*sem):
    return pltpu.CompilerParams(dimension_semantics=sem, vmem_limit_bytes=VMEM_LIMIT_BYTES)


def _tile(n, pref):
    if n <= pref:
        return n
    t = pref - pref % LANES
    while t > LANES and n % t:
        t -= LANES
    assert n % t == 0, (n, pref)
    return t


def _silu(v):
    return v * jax.nn.sigmoid(v)


def _dot_nt(a, b):
    return lax.dot_general(a, b, (((1,), (1,)), ((), ())), preferred_element_type=F32)


def _resident(shape, index_map):
    return pl.BlockSpec(shape, index_map)


def _cond_kernel(c_ref, wc_ref, bc_ref, wm_ref, bm_ref, o_ref):
    c = c_ref[...]
    t_row = jnp.sum(_silu(c) * wc_ref[...], axis=0, keepdims=True) + bc_ref[...]
    r = t_row.shape[1]
    eye = lax.broadcasted_iota(jnp.int32, (r, r), 0) == lax.broadcasted_iota(jnp.int32, (r, r), 1)
    t_col = jnp.sum(jnp.where(eye, jnp.broadcast_to(t_row, (r, r)), 0.0), axis=1, keepdims=True)
    o_ref[0] = jnp.sum(t_col * wm_ref[0], axis=0, keepdims=True) + bm_ref[0]


def _cond(c, w_cond, b_cond, w_mod, b_mod):
    depth, r, nd = w_mod.shape
    d = w_cond.shape[0]
    n_mod = nd // d
    return pl.pallas_call(
        _cond_kernel,
        grid=(depth, n_mod),
        in_specs=[
            pl.BlockSpec((d, 1), lambda l, j: (0, 0)),
            pl.BlockSpec((d, r), lambda l, j: (0, 0)),
            pl.BlockSpec((1, r), lambda l, j: (0, 0)),
            pl.BlockSpec((1, r, d), lambda l, j: (l, 0, j)),
            pl.BlockSpec((1, 1, d), lambda l, j: (l * n_mod + j, 0, 0)),
        ],
        out_specs=pl.BlockSpec((1, 1, d), lambda l, j: (l * n_mod + j, 0, 0)),
        out_shape=jax.ShapeDtypeStruct((depth * n_mod, 1, d), F32),
        compiler_params=_params("arbitrary", "arbitrary"),
        name="cond",
    )(c.reshape(d, 1), w_cond, b_cond.reshape(1, r), w_mod, b_mod.reshape(depth * n_mod, 1, d))


def _norm_body(x_ref, w_ref, sc_ref, sh_ref):
    x = x_ref[...]
    y = x * lax.rsqrt(jnp.mean(x * x, axis=-1, keepdims=True) + EPS)
    return (y * w_ref[...]) * (1.0 + sc_ref[0]) + sh_ref[0]


def _norm_kernel(x_ref, w_ref, sc_ref, sh_ref, o_ref):
    o_ref[...] = _norm_body(x_ref, w_ref, sc_ref, sh_ref).astype(o_ref.dtype)


def _norm_dt_kernel(x_ref, w_ref, sc_ref, sh_ref, wdt_ref, o_ref, dt_ref, *, heads):
    h = _norm_body(x_ref, w_ref, sc_ref, sh_ref).astype(o_ref.dtype)
    o_ref[...] = h
    dt = _dot_nt(h, wdt_ref[0].astype(BF16))
    lane = lax.broadcasted_iota(jnp.int32, (1, dt.shape[1]), 1)
    dt_ref[...] = jnp.where(lane < heads, dt, 0.0)


def _norm(x, w, mods, sc_idx, sh_idx, out_dtype, wt=None, layer=None, row0=None, heads=None):
    s, d = x.shape
    tm = _tile(s, 256)
    in_specs = [
        pl.BlockSpec((tm, d), lambda i: (i, 0)),
        pl.BlockSpec((1, d), lambda i: (0, 0)),
        pl.BlockSpec((1, 1, d), lambda i: (sc_idx, 0, 0)),
        pl.BlockSpec((1, 1, d), lambda i: (sh_idx, 0, 0)),
    ]
    args = [x, w.reshape(1, d), mods, mods]
    h_spec = pl.BlockSpec((tm, d), lambda i: (i, 0))
    h_shape = jax.ShapeDtypeStruct((s, d), out_dtype)
    if wt is None:
        return pl.pallas_call(
            _norm_kernel, grid=(s // tm,), in_specs=in_specs, out_specs=h_spec, out_shape=h_shape,
            compiler_params=_params("arbitrary"), name="norm")(*args)
    assert row0 + LANES <= wt.shape[1]
    wdt_spec = pl.BlockSpec((pl.Element(1), pl.Element(LANES), pl.Element(d)), lambda i: (layer, row0, 0))
    return pl.pallas_call(
        functools.partial(_norm_dt_kernel, heads=heads), grid=(s // tm,),
        in_specs=in_specs + [wdt_spec],
        out_specs=[h_spec, pl.BlockSpec((tm, LANES), lambda i: (i, 0))],
        out_shape=[h_shape, jax.ShapeDtypeStruct((s, LANES), F32)],
        compiler_params=_params("arbitrary"), name="norm_dt")(*args, wt)


def _proj_kernel(x_ref, w_ref, o_ref):
    w = w_ref[0].astype(BF16)
    tr = min(MXU_ROWS, x_ref.shape[0])
    for r0 in range(0, x_ref.shape[0], tr):
        rows = slice(r0, r0 + tr)
        o_ref[rows, :] = _dot_nt(x_ref[rows, :], w).astype(o_ref.dtype)


def _proj(x, wt, layer, segments):
    m, k = x.shape
    tm, tn = _tile(m, 2048), MXU_COLS
    assert all(rows % tn == 0 and row0 % HALO == 0 for row0, rows in segments)
    n = sum(rows for _, rows in segments)

    def row_of(j):
        r, t0 = None, 0
        for row0, rows in segments:
            here = row0 + (j - t0) * tn
            r = here if r is None else jnp.where(j >= t0, here, r)
            t0 += rows // tn
        return pl.multiple_of(r, HALO)

    return pl.pallas_call(
        _proj_kernel,
        grid=(m // tm, n // tn),
        in_specs=[
            _resident((tm, k), lambda i, j: (i, 0)),
            pl.BlockSpec((pl.Element(1), pl.Element(tn), pl.Element(k)), lambda i, j: (layer, row_of(j), 0)),
        ],
        out_specs=pl.BlockSpec((tm, tn), lambda i, j: (i, j)),
        out_shape=jax.ShapeDtypeStruct((m, n), BF16),
        compiler_params=_params("arbitrary", "arbitrary"), name="proj")(x, wt)


def _swiglu_kernel(x_ref, wg_ref, wu_ref, wd_ref, o_ref, wdo_ref):
    wdo_ref[...] = wd_ref[0].astype(wdo_ref.dtype)
    wg = wg_ref[0].astype(BF16)
    wu = wu_ref[0].astype(BF16)
    tr = min(MXU_ROWS, x_ref.shape[0])
    for r0 in range(0, x_ref.shape[0], tr):
        rows = slice(r0, r0 + tr)
        x = x_ref[rows, :]
        g = jnp.dot(x, wg, preferred_element_type=F32)
        u = jnp.dot(x, wu, preferred_element_type=F32)
        o_ref[rows, :] = (_silu(g) * u).astype(o_ref.dtype)


def _swiglu_in(x, w, w_down, layer):
    m, k = x.shape
    d_ff, n_out = w_down.shape[1:]
    tm, tn = _tile(m, 2048), 256
    assert d_ff % tn == 0
    nj = d_ff // tn
    steps = (m // tm) * nj
    slab = d_ff // steps
    assert slab * steps == d_ff and slab % 16 == 0
    return pl.pallas_call(
        _swiglu_kernel,
        grid=(m // tm, nj),
        in_specs=[
            _resident((tm, k), lambda i, j: (i, 0)),
            pl.BlockSpec((1, k, tn), lambda i, j: (layer, 0, j)),
            pl.BlockSpec((1, k, tn), lambda i, j: (layer, 0, nj + j)),
            pl.BlockSpec((1, slab, n_out), lambda i, j: (layer, i * nj + j, 0)),
        ],
        out_specs=[
            pl.BlockSpec((tm, tn), lambda i, j: (i, j)),
            pl.BlockSpec((slab, n_out), lambda i, j: (i * nj + j, 0)),
        ],
        out_shape=[jax.ShapeDtypeStruct((m, d_ff), BF16), jax.ShapeDtypeStruct((d_ff, n_out), BF16)],
        compiler_params=_params("arbitrary", "arbitrary"), name="swiglu_in")(x, w, w, w_down)


def _mm_res_kernel(a_ref, w_ref, r_ref, g_ref, o_ref, *, n_k, k_last):
    kk = pl.program_id(2)
    tk = a_ref.shape[1]
    tc = min(MXU_COLS, o_ref.shape[1])

    def step(first, kv):
        for c0 in range(0, o_ref.shape[1], tc):
            cols = slice(c0, c0 + tc)
            part = g_ref[0, :, cols] * jnp.dot(a_ref[:, 0:kv], w_ref[0:kv, cols], preferred_element_type=F32)
            o_ref[:, cols] = (r_ref[:, cols] if first else o_ref[:, cols]) + part

    if n_k == 1:
        step(True, k_last)
    else:
        pl.when(kk == 0)(functools.partial(step, True, tk))
        if k_last == tk:
            pl.when(kk > 0)(functools.partial(step, False, tk))
        else:
            pl.when((kk > 0) & (kk < n_k - 1))(functools.partial(step, False, tk))
            pl.when(kk == n_k - 1)(functools.partial(step, False, k_last))


def _matmul_residual(a, w, res, mods, g_idx, name):
    m, k = a.shape
    n = w.shape[1]
    tm, tn = _tile(m, 1024), _tile(n, 1024)
    n_k = -(-k // K_SPLIT)
    tk = -(-k // (n_k * 2 * LANES)) * 2 * LANES
    k_last = k - (n_k - 1) * tk
    assert 0 < k_last <= tk and k_last % LANES == 0
    return pl.pallas_call(
        functools.partial(_mm_res_kernel, n_k=n_k, k_last=k_last),
        grid=(m // tm, n // tn, n_k),
        in_specs=[
            pl.BlockSpec((tm, tk), lambda i, j, q: (i, q)),
            pl.BlockSpec((tk, tn), lambda i, j, q: (q, j)),
            pl.BlockSpec((tm, tn), lambda i, j, q: (i, j)),
            pl.BlockSpec((1, 1, tn), lambda i, j, q: (g_idx, 0, j)),
        ],
        out_specs=pl.BlockSpec((tm, tn), lambda i, j, q: (i, j)),
        out_shape=jax.ShapeDtypeStruct((m, n), F32),
        compiler_params=_params("arbitrary", "arbitrary", "arbitrary"), name=name)(a, w, res, mods)


def _mm_res_full_kernel(a_ref, w_ref, r_ref, g_ref, o_ref):
    w = w_ref[0].astype(BF16)
    tr = min(MXU_ROWS, a_ref.shape[0])
    for r0 in range(0, a_ref.shape[0], tr):
        rows = slice(r0, r0 + tr)
        o_ref[rows, :] = r_ref[rows, :] + g_ref[0] * jnp.dot(a_ref[rows, :], w, preferred_element_type=F32)


def _matmul_residual_full(a, w, layer, res, mods, g_idx, name):
    m, k = a.shape
    n = w.shape[-1]
    tm, tn = _tile(m, 2048), _tile(n, 256)
    return pl.pallas_call(
        _mm_res_full_kernel,
        grid=(m // tm, n // tn),
        in_specs=[
            _resident((tm, k), lambda i, j: (i, 0)),
            pl.BlockSpec((1, k, tn), lambda i, j: (layer, 0, j)),
            pl.BlockSpec((tm, tn), lambda i, j: (i, j)),
            pl.BlockSpec((1, 1, tn), lambda i, j: (g_idx, 0, j)),
        ],
        out_specs=pl.BlockSpec((tm, tn), lambda i, j: (i, j)),
        out_shape=jax.ShapeDtypeStruct((m, n), F32),
        compiler_params=_params("arbitrary", "arbitrary"), name=name)(a, w, res, mods)


def _merge_kernel(ya_ref, yb_ref, h_ref, wa_ref, wb_ref, wg_ref, bg_ref, o_ref):
    wa = wa_ref[0].astype(BF16)
    wb = wb_ref[0].astype(BF16)
    wg0 = wg_ref[0, 0, 0].astype(BF16)
    wg1 = wg_ref[0, 1, 0].astype(BF16)
    tr = min(MXU_ROWS, ya_ref.shape[0])
    for r0 in range(0, ya_ref.shape[0], tr):
        rows = slice(r0, r0 + tr)
        hk = h_ref[rows, :]
        pa = jnp.dot(ya_ref[rows, :], wa, preferred_element_type=F32)
        pb = jnp.dot(yb_ref[rows, :], wb, preferred_element_type=F32)
        g0 = jax.nn.sigmoid(jnp.dot(hk, wg0, preferred_element_type=F32) + bg_ref[0, 0:1, :])
        g1 = jax.nn.sigmoid(jnp.dot(hk, wg1, preferred_element_type=F32) + bg_ref[0, 1:2, :])
        o_ref[rows, :] = (g0 * pa + g1 * pb).astype(o_ref.dtype)


def _merge(ya, yb, h, wa, wb, wg, bg, layer):
    m, ka = ya.shape
    kb = yb.shape[1]
    n = wa.shape[-1]
    gbd = wg.shape[-1]
    tm, tn = _tile(m, 2048), gbd
    return pl.pallas_call(
        _merge_kernel,
        grid=(m // tm, n // tn),
        in_specs=[
            _resident((tm, ka), lambda i, j: (i, 0)),
            _resident((tm, kb), lambda i, j: (i, 0)),
            pl.BlockSpec((tm, tn), lambda i, j: (i, j)),
            pl.BlockSpec((1, ka, tn), lambda i, j: (layer, 0, j)),
            pl.BlockSpec((1, kb, tn), lambda i, j: (layer, 0, j)),
            pl.BlockSpec((1, 2, 1, gbd, gbd), lambda i, j: (layer, 0, j, 0, 0)),
            pl.BlockSpec((1, 2, tn), lambda i, j: (layer, 0, j)),
        ],
        out_specs=pl.BlockSpec((tm, tn), lambda i, j: (i, j)),
        out_shape=jax.ShapeDtypeStruct((m, n), BF16),
        compiler_params=_params("arbitrary", "arbitrary"), name="merge")(ya, yb, h, wa, wb, wg, bg)


def _split(v, parts):
    out = []
    for _ in range(parts):
        p = v.astype(BF16)
        out.append(p)
        v = v - p.astype(F32)
    return out


def _dot_split(lhs_parts, rhs):
    acc = None
    for p in lhs_parts:
        t = jnp.dot(p, rhs, preferred_element_type=F32)
        acc = t if acc is None else acc + t
    return acc


def _ssd_kernel(xbc_ref, z_ref, dtr_ref, sc_ref, cw_ref, cb_ref, dtb_ref, alog_ref, dskip_ref, nw_ref,
                scw_ref, e_ref, ya_ref, yb_ref, ext_ref, u_ref, ext2_ref, st_ref, *, groups, head_dim):
    q = xbc_ref.shape[0]
    cd = xbc_ref.shape[1]
    ds = z_ref.shape[1]
    dc = yb_ref.shape[1]
    n = (cd - ds) // (2 * groups)
    rp = ds // groups
    kc = cw_ref.shape[0]
    ks = scw_ref.shape[0]
    cblk = 512

    @pl.when(pl.program_id(0) == 0)
    def _():
        ext_ref[0:HALO, :] = jnp.zeros((HALO, cd), F32)
        ext2_ref[0:HALO, :] = jnp.zeros((HALO, dc), F32)
        st_ref[...] = jnp.zeros(st_ref.shape, F32)

    ext_ref[HALO:HALO + q, :] = xbc_ref[...].astype(F32)
    for c0 in range(0, cd, cblk):
        cols = slice(c0, c0 + cblk)
        acc = cb_ref[:, cols] + cw_ref[kc - 1:kc, cols] * ext_ref[HALO:HALO + q, cols]
        for j in range(kc - 1):
            r0 = HALO - (kc - 1) + j
            acc = acc + cw_ref[j:j + 1, cols] * ext_ref[r0:r0 + q, cols]
        u_ref[:, cols] = _silu(acc)
    ext_ref[0:HALO, :] = ext_ref[q:q + HALO, :]

    ext2_ref[HALO:HALO + q, :] = sc_ref[:, 2 * dc:3 * dc].astype(F32) * sc_ref[:, 0:dc].astype(F32)
    for c0 in range(0, dc, cblk):
        cols = slice(c0, c0 + cblk)
        acc = scw_ref[ks - 1:ks, cols] * ext2_ref[HALO:HALO + q, cols]
        for j in range(ks - 1):
            r0 = HALO - (ks - 1) + j
            acc = acc + scw_ref[j:j + 1, cols] * ext2_ref[r0:r0 + q, cols]
        yb_ref[:, cols] = (sc_ref[:, dc + c0:dc + c0 + cblk].astype(F32) * acc).astype(yb_ref.dtype)
    ext2_ref[0:HALO, :] = ext2_ref[q:q + HALO, :]

    x_dt = dtr_ref[...] + dtb_ref[...]
    dt = jnp.maximum(x_dt, 0.0) + jnp.log1p(jnp.exp(-jnp.abs(x_dt)))
    da = dt * (-jnp.exp(alog_ref[...]))
    row = lax.broadcasted_iota(jnp.int32, (q, q), 0)
    col = lax.broadcasted_iota(jnp.int32, (q, q), 1)
    causal = row >= col
    tri = causal.astype(BF16)
    acs = None
    for p in _split(da, 3):
        t = jnp.dot(tri, p, preferred_element_type=F32)
        acs = t if acs is None else acs + t
    acs_t = jnp.transpose(acs)
    e_acs = jnp.exp(acs)
    dt_end = dt * jnp.exp(acs[q - 1:q, :] - acs)

    e = e_ref[...]
    dt_x = _dot_split(_split(dt, 2), e)
    eacs_x = _dot_split(_split(e_acs, 2), e)
    dtend_x = _dot_split(_split(dt_end, 2), e)

    lane = lax.broadcasted_iota(jnp.int32, (1, LANES), 1)
    even = (lane < head_dim).astype(F32)
    odd = 1.0 - even
    r_heads = rp // head_dim

    for g in range(groups):
        gc = slice(g * rp, (g + 1) * rp)
        xs = u_ref[:, gc]
        bb = u_ref[:, ds + g * n:ds + (g + 1) * n].astype(BF16)
        cbf = u_ref[:, ds + (groups + g) * n:ds + (groups + g + 1) * n].astype(BF16)
        cb_mat = _dot_nt(cbf, bb)
        xd = xs * dt_x[:, gc]
        st = st_ref[g]
        y_off = jnp.dot(cbf, st.astype(BF16), preferred_element_type=F32) * eacs_x[:, gc]
        pieces = []
        for pr in range(rp // LANES):
            pc = slice(pr * LANES, (pr + 1) * LANES)
            xd_p = xd[:, pc]
            lhs, rhs = [], []
            for half, msk in ((0, even), (1, odd)):
                hh = g * r_heads + pr * (LANES // head_dim) + half
                seg = acs[:, hh:hh + 1] - acs_t[hh:hh + 1, :]
                lm = jnp.exp(jnp.where(causal, seg, -jnp.inf))
                lhs.append((cb_mat * lm).astype(BF16))
                rhs.append((xd_p * msk).astype(BF16))
            pieces.append(jnp.dot(jnp.concatenate(lhs, axis=1), jnp.concatenate(rhs, axis=0),
                                  preferred_element_type=F32))
        y_diag = jnp.concatenate(pieces, axis=1) if len(pieces) > 1 else pieces[0]
        y = y_diag + y_off + dskip_ref[:, gc] * xs
        y = y * _silu(z_ref[:, gc].astype(F32))
        yn = y * lax.rsqrt(jnp.mean(y * y, axis=-1, keepdims=True) + EPS)
        ya_ref[:, gc] = (yn * nw_ref[:, gc]).astype(ya_ref.dtype)
        xde = (xs * dtend_x[:, gc]).astype(BF16)
        contrib = lax.dot_general(bb, xde, (((0,), (0,)), ((), ())), preferred_element_type=F32)
        st_ref[g] = st * eacs_x[q - 1:q, gc] + contrib


def _ssd_call(proj, dtr, cw, cb, dtb, alog, dskip, nw, scw, e, ds, cd, dc, head_dim):
    s = proj.shape[0]
    q = SSD_CHUNK
    groups = SSD_GROUPS
    n = (cd - ds) // (2 * groups)
    rp = ds // groups
    assert s % q == 0 and rp % LANES == 0 and 2 * head_dim == LANES and n % LANES == 0
    assert cd % 512 == 0 and dc % 512 == 0 and ds % LANES == 0
    window = lambda c0, w: pl.BlockSpec((pl.Element(q), pl.Element(w)), lambda c: (c * q, c0))
    row = lambda w: pl.BlockSpec((q, w), lambda c: (c, 0))
    full = lambda a: pl.BlockSpec(a.shape, lambda c: (0,) * a.ndim)
    params = [cw, cb, dtb, alog, dskip, nw, scw, e]
    return pl.pallas_call(
        functools.partial(_ssd_kernel, groups=groups, head_dim=head_dim),
        grid=(s // q,),
        in_specs=[window(ds, cd), window(0, ds), row(dtr.shape[1]), window(ds + cd, 3 * dc)]
        + [full(a) for a in params],
        out_specs=[row(ds), row(dc)],
        out_shape=[jax.ShapeDtypeStruct((s, ds), BF16), jax.ShapeDtypeStruct((s, dc), BF16)],
        scratch_shapes=[
            pltpu.VMEM((q + HALO, cd), F32),
            pltpu.VMEM((q, cd), F32),
            pltpu.VMEM((q + HALO, dc), F32),
            pltpu.VMEM((groups, n, rp), F32),
        ],
        compiler_params=_params("arbitrary"), name="ssd")(proj, proj, dtr, proj, *params)


def kernel(x, c, w_cond, b_cond, w_mod, b_mod, norm_mix_w, w_in, conv_ssd_w, conv_ssd_b, dt_bias, a_log,
           d_skip, ssd_norm_w, sc_conv_w, w_gate, b_gate, w_br_ssd, w_br_sc, w_o, norm_ffn_w, w_ffn_in,
           w_ffn_out, final_norm_w):
    b, s, d = x.shape
    assert b == 1, "modulation rows are shared by every token: one batch element only"
    depth = w_mod.shape[0]
    heads = dt_bias.shape[1]
    ds = ssd_norm_w.shape[1]
    cd = conv_ssd_w.shape[2]
    dc = sc_conv_w.shape[2]
    head_dim = ds // heads
    assert heads <= LANES

    mods = _cond(c, w_cond, b_cond, w_mod, b_mod)
    zero_row = jnp.zeros((1, 1, d), F32)
    mods = jnp.concatenate([mods, zero_row], axis=0)

    e = (jnp.arange(LANES)[:, None] == (jnp.arange(ds)[None, :] // head_dim)).astype(BF16)
    pad_h = lambda v: jnp.pad(v.astype(F32), (0, LANES - heads)).reshape(1, LANES)

    w_in_t = jnp.swapaxes(w_in, 1, 2)
    sc0 = ds + cd + heads

    xf = x.reshape(s, d)
    for i in range(depth):
        m0 = i * N_MOD
        h, dtr = _norm(xf, norm_mix_w[i], mods, m0 + 1, m0 + 0, BF16, wt=w_in_t, layer=i, row0=ds + cd, heads=heads)
        proj = _proj(h, w_in_t, i, [(0, ds + cd), (sc0, 3 * dc)])
        ya, yb = _ssd_call(
            proj, dtr, conv_ssd_w[i], conv_ssd_b[i].reshape(1, cd), pad_h(dt_bias[i]), pad_h(a_log[i]),
            jnp.repeat(d_skip[i], head_dim).reshape(1, ds), ssd_norm_w[i].reshape(1, ds), sc_conv_w[i], e,
            ds, cd, dc, head_dim)
        merged = _merge(ya, yb, h, w_br_ssd, w_br_sc, w_gate, b_gate, i)
        xf = _matmul_residual_full(merged, w_o, i, xf, mods, m0 + 2, "out_proj")

        h2 = _norm(xf, norm_ffn_w[i], mods, m0 + 4, m0 + 3, BF16)
        act, w_dn = _swiglu_in(h2, w_ffn_in, w_ffn_out, i)
        xf = _matmul_residual(act, w_dn, xf, mods, m0 + 5, "ffn_out")

    out = _norm(xf, final_norm_w, mods, depth * N_MOD, depth * N_MOD, x.dtype)
    return out.reshape(b, s, d)
```

```python
import functools

import jax
import jax.numpy as jnp
from jax import lax
from jax.experimental import pallas as pl
from jax.experimental.pallas import tpu as pltpu

SSD_GROUPS = 8
SSD_CHUNK = 128
N_MOD = 6
EPS = 1e-6
LANES = 128
HALO = 8
VMEM_LIMIT_BYTES = 61 * 1024 * 1024
K_SPLIT = 4096
MXU_ROWS = 1024
MXU_COLS = 512

F32 = jnp.float32
BF16 = jnp.bfloat16


def _params(*sem):
    return pltpu.CompilerParams(dimension_semantics=sem, vmem_limit_bytes=VMEM_LIMIT_BYTES)


def _tile(n, pref):
    if n <= pref:
        return n
    t = pref - pref % LANES
    while t > LANES and n % t:
        t -= LANES
    assert n % t == 0, (n, pref)
    return t


def _silu(v):
    return v * jax.nn.sigmoid(v)


def _dot_nt(a, b):
    return lax.dot_general(a, b, (((1,), (1,)), ((), ())), preferred_element_type=F32)


def _resident(shape, index_map):
    return pl.BlockSpec(shape, index_map)


def _cond_kernel(c_ref, wc_ref, bc_ref, wm_ref, bm_ref, o_ref):
    c = c_ref[...]
    t_row = jnp.sum(_silu(c) * wc_ref[...], axis=0, keepdims=True) + bc_ref[...]
    r = t_row.shape[1]
    eye = lax.broadcasted_iota(jnp.int32, (r, r), 0) == lax.broadcasted_iota(jnp.int32, (r, r), 1)
    t_col = jnp.sum(jnp.where(eye, jnp.broadcast_to(t_row, (r, r)), 0.0), axis=1, keepdims=True)
    o_ref[0] = jnp.sum(t_col * wm_ref[0], axis=0, keepdims=True) + bm_ref[0]


def _cond(c, w_cond, b_cond, w_mod, b_mod):
    depth, r, nd = w_mod.shape
    d = w_cond.shape[0]
    n_mod = nd // d
    return pl.pallas_call(
        _cond_kernel,
        grid=(depth, n_mod),
        in_specs=[
            pl.BlockSpec((d, 1), lambda l, j: (0, 0)),
            pl.BlockSpec((d, r), lambda l, j: (0, 0)),
            pl.BlockSpec((1, r), lambda l, j: (0, 0)),
            pl.BlockSpec((1, r, d), lambda l, j: (l, 0, j)),
            pl.BlockSpec((1, 1, d), lambda l, j: (l * n_mod + j, 0, 0)),
        ],
        out_specs=pl.BlockSpec((1, 1, d), lambda l, j: (l * n_mod + j, 0, 0)),
        out_shape=jax.ShapeDtypeStruct((depth * n_mod, 1, d), F32),
        compiler_params=_params("arbitrary", "arbitrary"),
        name="cond",
    )(c.reshape(d, 1), w_cond, b_cond.reshape(1, r), w_mod, b_mod.reshape(depth * n_mod, 1, d))


def _norm_body(x_ref, w_ref, sc_ref, sh_ref):
    x = x_ref[...]
    y = x * lax.rsqrt(jnp.mean(x * x, axis=-1, keepdims=True) + EPS)
    return (y * w_ref[...]) * (1.0 + sc_ref[0]) + sh_ref[0]


def _norm_kernel(x_ref, w_ref, sc_ref, sh_ref, o_ref):
    o_ref[...] = _norm_body(x_ref, w_ref, sc_ref, sh_ref).astype(o_ref.dtype)


def _norm_dt_kernel(x_ref, w_ref, sc_ref, sh_ref, wdt_ref, o_ref, dt_ref, *, heads):
    h = _norm_body(x_ref, w_ref, sc_ref, sh_ref).astype(o_ref.dtype)
    o_ref[...] = h
    dt = _dot_nt(h, wdt_ref[0].astype(BF16))
    lane = lax.broadcasted_iota(jnp.int32, (1, dt.shape[1]), 1)
    dt_ref[...] = jnp.where(lane < heads, dt, 0.0)


def _norm(x, w, mods, sc_idx, sh_idx, out_dtype, wt=None, layer=None, row0=None, heads=None):
    s, d = x.shape
    tm = _tile(s, 256)
    in_specs = [
        pl.BlockSpec((tm, d), lambda i: (i, 0)),
        pl.BlockSpec((1, d), lambda i: (0, 0)),
        pl.BlockSpec((1, 1, d), lambda i: (sc_idx, 0, 0)),
        pl.BlockSpec((1, 1, d), lambda i: (sh_idx, 0, 0)),
    ]
    args = [x, w.reshape(1, d), mods, mods]
    h_spec = pl.BlockSpec((tm, d), lambda i: (i, 0))
    h_shape = jax.ShapeDtypeStruct((s, d), out_dtype)
    if wt is None:
        return pl.pallas_call(
            _norm_kernel, grid=(s // tm,), in_specs=in_specs, out_specs=h_spec, out_shape=h_shape,
            compiler_params=_params("arbitrary"), name="norm")(*args)
    assert row0 + LANES <= wt.shape[1]
    wdt_spec = pl.BlockSpec((pl.Element(1), pl.Element(LANES), pl.Element(d)), lambda i: (layer, row0, 0))
    return pl.pallas_call(
        functools.partial(_norm_dt_kernel, heads=heads), grid=(s // tm,),
        in_specs=in_specs + [wdt_spec],
        out_specs=[h_spec, pl.BlockSpec((tm, LANES), lambda i: (i, 0))],
        out_shape=[h_shape, jax.ShapeDtypeStruct((s, LANES), F32)],
        compiler_params=_params("arbitrary"), name="norm_dt")(*args, wt)


def _proj_kernel(x_ref, w_ref, o_ref):
    w = w_ref[0].astype(BF16)
    tr = min(MXU_ROWS, x_ref.shape[0])
    for r0 in range(0, x_ref.shape[0], tr):
        rows = slice(r0, r0 + tr)
        o_ref[rows, :] = _dot_nt(x_ref[rows, :], w).astype(o_ref.dtype)


def _proj(x, wt, layer, segments):
    m, k = x.shape
    tm, tn = _tile(m, 2048), MXU_COLS
    assert all(rows % tn == 0 and row0 % HALO == 0 for row0, rows in segments)
    n = sum(rows for _, rows in segments)

    def row_of(j):
        r, t0 = None, 0
        for row0, rows in segments:
            here = row0 + (j - t0) * tn
            r = here if r is None else jnp.where(j >= t0, here, r)
            t0 += rows // tn
        return pl.multiple_of(r, HALO)

    return pl.pallas_call(
        _proj_kernel,
        grid=(m // tm, n // tn),
        in_specs=[
            _resident((tm, k), lambda i, j: (i, 0)),
            pl.BlockSpec((pl.Element(1), pl.Element(tn), pl.Element(k)), lambda i, j: (layer, row_of(j), 0)),
        ],
        out_specs=pl.BlockSpec((tm, tn), lambda i, j: (i, j)),
        out_shape=jax.ShapeDtypeStruct((m, n), BF16),
        compiler_params=_params("arbitrary", "arbitrary"), name="proj")(x, wt)


def _swiglu_kernel(x_ref, wg_ref, wu_ref, wd_ref, o_ref, wdo_ref):
    wdo_ref[...] = wd_ref[0].astype(wdo_ref.dtype)
    wg = wg_ref[0].astype(BF16)
    wu = wu_ref[0].astype(BF16)
    tr = min(MXU_ROWS, x_ref.shape[0])
    for r0 in range(0, x_ref.shape[0], tr):
        rows = slice(r0, r0 + tr)
        x = x_ref[rows, :]
        g = jnp.dot(x, wg, preferred_element_type=F32)
        u = jnp.dot(x, wu, preferred_element_type=F32)
        o_ref[rows, :] = (_silu(g) * u).astype(o_ref.dtype)


def _swiglu_in(x, w, w_down, layer):
    m, k = x.shape
    d_ff, n_out = w_down.shape[1:]
    tm, tn = _tile(m, 2048), 256
    assert d_ff % tn == 0
    nj = d_ff // tn
    steps = (m // tm) * nj
    slab = d_ff // steps
    assert slab * steps == d_ff and slab % 16 == 0
    return pl.pallas_call(
        _swiglu_kernel,
        grid=(m // tm, nj),
        in_specs=[
            _resident((tm, k), lambda i, j: (i, 0)),
            pl.BlockSpec((1, k, tn), lambda i, j: (layer, 0, j)),
            pl.BlockSpec((1, k, tn), lambda i, j: (layer, 0, nj + j)),
            pl.BlockSpec((1, slab, n_out), lambda i, j: (layer, i * nj + j, 0)),
        ],
        out_specs=[
            pl.BlockSpec((tm, tn), lambda i, j: (i, j)),
            pl.BlockSpec((slab, n_out), lambda i, j: (i * nj + j, 0)),
        ],
        out_shape=[jax.ShapeDtypeStruct((m, d_ff), BF16), jax.ShapeDtypeStruct((d_ff, n_out), BF16)],
        compiler_params=_params("arbitrary", "arbitrary"), name="swiglu_in")(x, w, w, w_down)


def _mm_res_kernel(a_ref, w_ref, r_ref, g_ref, o_ref, *, n_k, k_last):
    kk = pl.program_id(2)
    tk = a_ref.shape[1]
    tc = min(MXU_COLS, o_ref.shape[1])

    def step(first, kv):
        for c0 in range(0, o_ref.shape[1], tc):
            cols = slice(c0, c0 + tc)
            part = g_ref[0, :, cols] * jnp.dot(a_ref[:, 0:kv], w_ref[0:kv, cols], preferred_element_type=F32)
            o_ref[:, cols] = (r_ref[:, cols] if first else o_ref[:, cols]) + part

    if n_k == 1:
        step(True, k_last)
    else:
        pl.when(kk == 0)(functools.partial(step, True, tk))
        if k_last == tk:
            pl.when(kk > 0)(functools.partial(step, False, tk))
        else:
            pl.when((kk > 0) & (kk < n_k - 1))(functools.partial(step, False, tk))
            pl.when(kk == n_k - 1)(functools.partial(step, False, k_last))


def _matmul_residual(a, w, res, mods, g_idx, name):
    m, k = a.shape
    n = w.shape[1]
    tm, tn = _tile(m, 1024), _tile(n, 1024)
    n_k = -(-k // K_SPLIT)
    tk = -(-k // (n_k * 2 * LANES)) * 2 * LANES
    k_last = k - (n_k - 1) * tk
    assert 0 < k_last <= tk and k_last % LANES == 0
    return pl.pallas_call(
        functools.partial(_mm_res_kernel, n_k=n_k, k_last=k_last),
        grid=(m // tm, n // tn, n_k),
        in_specs=[
            pl.BlockSpec((tm, tk), lambda i, j, q: (i, q)),
            pl.BlockSpec((tk, tn), lambda i, j, q: (q, j)),
            pl.BlockSpec((tm, tn), lambda i, j, q: (i, j)),
            pl.BlockSpec((1, 1, tn), lambda i, j, q: (g_idx, 0, j)),
        ],
        out_specs=pl.BlockSpec((tm, tn), lambda i, j, q: (i, j)),
        out_shape=jax.ShapeDtypeStruct((m, n), F32),
        compiler_params=_params("arbitrary", "arbitrary", "arbitrary"), name=name)(a, w, res, mods)


def _mm_res_full_kernel(a_ref, w_ref, r_ref, g_ref, o_ref):
    w = w_ref[0].astype(BF16)
    tr = min(MXU_ROWS, a_ref.shape[0])
    for r0 in range(0, a_ref.shape[0], tr):
        rows = slice(r0, r0 + tr)
        o_ref[rows, :] = r_ref[rows, :] + g_ref[0] * jnp.dot(a_ref[rows, :], w, preferred_element_type=F32)


def _matmul_residual_full(a, w, layer, res, mods, g_idx, name):
    m, k = a.shape
    n = w.shape[-1]
    tm, tn = _tile(m, 2048), _tile(n, 256)
    return pl.pallas_call(
        _mm_res_full_kernel,
        grid=(m // tm, n // tn),
        in_specs=[
            _resident((tm, k), lambda i, j: (i, 0)),
            pl.BlockSpec((1, k, tn), lambda i, j: (layer, 0, j)),
            pl.BlockSpec((tm, tn), lambda i, j: (i, j)),
            pl.BlockSpec((1, 1, tn), lambda i, j: (g_idx, 0, j)),
        ],
        out_specs=pl.BlockSpec((tm, tn), lambda i, j: (i, j)),
        out_shape=jax.ShapeDtypeStruct((m, n), F32),
        compiler_params=_params("arbitrary", "arbitrary"), name=name)(a, w, res, mods)


def _merge_kernel(ya_ref, yb_ref, h_ref, wa_ref, wb_ref, wg_ref, bg_ref, o_ref):
    wa = wa_ref[0].astype(BF16)
    wb = wb_ref[0].astype(BF16)
    wg0 = wg_ref[0, 0, 0].astype(BF16)
    wg1 = wg_ref[0, 1, 0].astype(BF16)
    tr = min(MXU_ROWS, ya_ref.shape[0])
    for r0 in range(0, ya_ref.shape[0], tr):
        rows = slice(r0, r0 + tr)
        hk = h_ref[rows, :]
        pa = jnp.dot(ya_ref[rows, :], wa, preferred_element_type=F32)
        pb = jnp.dot(yb_ref[rows, :], wb, preferred_element_type=F32)
        g0 = jax.nn.sigmoid(jnp.dot(hk, wg0, preferred_element_type=F32) + bg_ref[0, 0:1, :])
        g1 = jax.nn.sigmoid(jnp.dot(hk, wg1, preferred_element_type=F32) + bg_ref[0, 1:2, :])
        o_ref[rows, :] = (g0 * pa + g1 * pb).astype(o_ref.dtype)


def _merge(ya, yb, h, wa, wb, wg, bg, layer):
    m, ka = ya.shape
    kb = yb.shape[1]
    n = wa.shape[-1]
    gbd = wg.shape[-1]
    tm, tn = _tile(m, 2048), gbd
    return pl.pallas_call(
        _merge_kernel,
        grid=(m // tm, n // tn),
        in_specs=[
            _resident((tm, ka), lambda i, j: (i, 0)),
            _resident((tm, kb), lambda i, j: (i, 0)),
            pl.BlockSpec((tm, tn), lambda i, j: (i, j)),
            pl.BlockSpec((1, ka, tn), lambda i, j: (layer, 0, j)),
            pl.BlockSpec((1, kb, tn), lambda i, j: (layer, 0, j)),
            pl.BlockSpec((1, 2, 1, gbd, gbd), lambda i, j: (layer, 0, j, 0, 0)),
            pl.BlockSpec((1, 2, tn), lambda i, j: (layer, 0, j)),
        ],
        out_specs=pl.BlockSpec((tm, tn), lambda i, j: (i, j)),
        out_shape=jax.ShapeDtypeStruct((m, n), BF16),
        compiler_params=_params("arbitrary", "arbitrary"), name="merge")(ya, yb, h, wa, wb, wg, bg)


def _split(v, parts):
    out = []
    for _ in range(parts):
        p = v.astype(BF16)
        out.append(p)
        v = v - p.astype(F32)
    return out


def _dot_split(lhs_parts, rhs):
    acc = None
    for p in lhs_parts:
        t = jnp.dot(p, rhs, preferred_element_type=F32)
        acc = t if acc is None else acc + t
    return acc


def _ssd_kernel(xbc_ref, z_ref, dtr_ref, sc_ref, cw_ref, cb_ref, dtb_ref, alog_ref, dskip_ref, nw_ref,
                scw_ref, e_ref, ya_ref, yb_ref, ext_ref, u_ref, ext2_ref, st_ref, *, groups, head_dim):
    q = xbc_ref.shape[0]
    cd = xbc_ref.shape[1]
    ds = z_ref.shape[1]
    dc = yb_ref.shape[1]
    n = (cd - ds) // (2 * groups)
    rp = ds // groups
    kc = cw_ref.shape[0]
    ks = scw_ref.shape[0]
    cblk = 512

    @pl.when(pl.program_id(0) == 0)
    def _():
        ext_ref[0:HALO, :] = jnp.zeros((HALO, cd), F32)
        ext2_ref[0:HALO, :] = jnp.zeros((HALO, dc), F32)
        st_ref[...] = jnp.zeros(st_ref.shape, F32)

    ext_ref[HALO:HALO + q, :] = xbc_ref[...].astype(F32)
    for c0 in range(0, cd, cblk):
        cols = slice(c0, c0 + cblk)
        acc = cb_ref[:, cols] + cw_ref[kc - 1:kc, cols] * ext_ref[HALO:HALO + q, cols]
        for j in range(kc - 1):
            r0 = HALO - (kc - 1) + j
            acc = acc + cw_ref[j:j + 1, cols] * ext_ref[r0:r0 + q, cols]
        u_ref[:, cols] = _silu(acc)
    ext_ref[0:HALO, :] = ext_ref[q:q + HALO, :]

    ext2_ref[HALO:HALO + q, :] = sc_ref[:, 2 * dc:3 * dc].astype(F32) * sc_ref[:, 0:dc].astype(F32)
    for c0 in range(0, dc, cblk):
        cols = slice(c0, c0 + cblk)
        acc = scw_ref[ks - 1:ks, cols] * ext2_ref[HALO:HALO + q, cols]
        for j in range(ks - 1):
            r0 = HALO - (ks - 1) + j
            acc = acc + scw_ref[j:j + 1, cols] * ext2_ref[r0:r0 + q, cols]
        yb_ref[:, cols] = (sc_ref[:, dc + c0:dc + c0 + cblk].astype(F32) * acc).astype(yb_ref.dtype)
    ext2_ref[0:HALO, :] = ext2_ref[q:q + HALO, :]

    x_dt = dtr_ref[...] + dtb_ref[...]
    dt = jnp.maximum(x_dt, 0.0) + jnp.log1p(jnp.exp(-jnp.abs(x_dt)))
    da = dt * (-jnp.exp(alog_ref[...]))
    row = lax.broadcasted_iota(jnp.int32, (q, q), 0)
    col = lax.broadcasted_iota(jnp.int32, (q, q), 1)
    causal = row >= col
    tri = causal.astype(BF16)
    acs = None
    for p in _split(da, 3):
        t = jnp.dot(tri, p, preferred_element_type=F32)
        acs = t if acs is None else acs + t
    acs_t = jnp.transpose(acs)
    e_acs = jnp.exp(acs)
    dt_end = dt * jnp.exp(acs[q - 1:q, :] - acs)

    e = e_ref[...]
    dt_x = _dot_split(_split(dt, 2), e)
    eacs_x = _dot_split(_split(e_acs, 2), e)
    dtend_x = _dot_split(_split(dt_end, 2), e)

    lane = lax.broadcasted_iota(jnp.int32, (1, LANES), 1)
    even = (lane < head_dim).astype(F32)
    odd = 1.0 - even
    r_heads = rp // head_dim

    for g in range(groups):
        gc = slice(g * rp, (g + 1) * rp)
        xs = u_ref[:, gc]
        bb = u_ref[:, ds + g * n:ds + (g + 1) * n].astype(BF16)
        cbf = u_ref[:, ds + (groups + g) * n:ds + (groups + g + 1) * n].astype(BF16)
        cb_mat = _dot_nt(cbf, bb)
        xd = xs * dt_x[:, gc]
        st = st_ref[g]
        y_off = jnp.dot(cbf, st.astype(BF16), preferred_element_type=F32) * eacs_x[:, gc]
        pieces = []
        for pr in range(rp // LANES):
            pc = slice(pr * LANES, (pr + 1) * LANES)
            xd_p = xd[:, pc]
            lhs, rhs = [], []
            for half, msk in ((0, even), (1, odd)):
                hh = g * r_heads + pr * (LANES // head_dim) + half
                seg = acs[:, hh:hh + 1] - acs_t[hh:hh + 1, :]
                lm = jnp.exp(jnp.where(causal, seg, -jnp.inf))
                lhs.append((cb_mat * lm).astype(BF16))
                rhs.append((xd_p * msk).astype(BF16))
            pieces.append(jnp.dot(jnp.concatenate(lhs, axis=1), jnp.concatenate(rhs, axis=0),
                                  preferred_element_type=F32))
        y_diag = jnp.concatenate(pieces, axis=1) if len(pieces) > 1 else pieces[0]
        y = y_diag + y_off + dskip_ref[:, gc] * xs
        y = y * _silu(z_ref[:, gc].astype(F32))
        yn = y * lax.rsqrt(jnp.mean(y * y, axis=-1, keepdims=True) + EPS)
        ya_ref[:, gc] = (yn * nw_ref[:, gc]).astype(ya_ref.dtype)
        xde = (xs * dtend_x[:, gc]).astype(BF16)
        contrib = lax.dot_general(bb, xde, (((0,), (0,)), ((), ())), preferred_element_type=F32)
        st_ref[g] = st * eacs_x[q - 1:q, gc] + contrib


def _ssd_call(proj, dtr, cw, cb, dtb, alog, dskip, nw, scw, e, ds, cd, dc, head_dim):
    s = proj.shape[0]
    q = SSD_CHUNK
    groups = SSD_GROUPS
    n = (cd - ds) // (2 * groups)
    rp = ds // groups
    assert s % q == 0 and rp % LANES == 0 and 2 * head_dim == LANES and n % LANES == 0
    assert cd % 512 == 0 and dc % 512 == 0 and ds % LANES == 0
    window = lambda c0, w: pl.BlockSpec((pl.Element(q), pl.Element(w)), lambda c: (c * q, c0))
    row = lambda w: pl.BlockSpec((q, w), lambda c: (c, 0))
    full = lambda a: pl.BlockSpec(a.shape, lambda c: (0,) * a.ndim)
    params = [cw, cb, dtb, alog, dskip, nw, scw, e]
    return pl.pallas_call(
        functools.partial(_ssd_kernel, groups=groups, head_dim=head_dim),
        grid=(s // q,),
        in_specs=[window(ds, cd), window(0, ds), row(dtr.shape[1]), window(ds + cd, 3 * dc)]
        + [full(a) for a in params],
        out_specs=[row(ds), row(dc)],
        out_shape=[jax.ShapeDtypeStruct((s, ds), BF16), jax.ShapeDtypeStruct((s, dc), BF16)],
        scratch_shapes=[
            pltpu.VMEM((q + HALO, cd), F32),
            pltpu.VMEM((q, cd), F32),
            pltpu.VMEM((q + HALO, dc), F32),
            pltpu.VMEM((groups, n, rp), F32),
        ],
        compiler_params=_params("arbitrary"), name="ssd")(proj, proj, dtr, proj, *params)


def kernel(x, c, w_cond, b_cond, w_mod, b_mod, norm_mix_w, w_in, conv_ssd_w, conv_ssd_b, dt_bias, a_log,
           d_skip, ssd_norm_w, sc_conv_w, w_gate, b_gate, w_br_ssd, w_br_sc, w_o, norm_ffn_w, w_ffn_in,
           w_ffn_out, final_norm_w):
    b, s, d = x.shape
    assert b == 1, "modulation rows are shared by every token: one batch element only"
    depth = w_mod.shape[0]
    heads = dt_bias.shape[1]
    ds = ssd_norm_w.shape[1]
    cd = conv_ssd_w.shape[2]
    dc = sc_conv_w.shape[2]
    head_dim = ds // heads
    assert heads <= LANES

    mods = _cond(c, w_cond, b_cond, w_mod, b_mod)
    zero_row = jnp.zeros((1, 1, d), F32)
    mods = jnp.concatenate([mods, zero_row], axis=0)

    e = (jnp.arange(LANES)[:, None] == (jnp.arange(ds)[None, :] // head_dim)).astype(BF16)
    pad_h = lambda v: jnp.pad(v.astype(F32), (0, LANES - heads)).reshape(1, LANES)

    w_in_t = jnp.swapaxes(w_in, 1, 2)
    sc0 = ds + cd + heads

    xf = x.reshape(s, d)
    for i in range(depth):
        m0 = i * N_MOD
        h, dtr = _norm(xf, norm_mix_w[i], mods, m0 + 1, m0 + 0, BF16, wt=w_in_t, layer=i, row0=ds + cd, heads=heads)
        proj = _proj(h, w_in_t, i, [(0, ds + cd), (sc0, 3 * dc)])
        ya, yb = _ssd_call(
            proj, dtr, conv_ssd_w[i], conv_ssd_b[i].reshape(1, cd), pad_h(dt_bias[i]), pad_h(a_log[i]),
            jnp.repeat(d_skip[i], head_dim).reshape(1, ds), ssd_norm_w[i].reshape(1, ds), sc_conv_w[i], e,
            ds, cd, dc, head_dim)
        merged = _merge(ya, yb, h, w_br_ssd, w_br_sc, w_gate, b_gate, i)
        xf = _matmul_residual_full(merged, w_o, i, xf, mods, m0 + 2, "out_proj")

        h2 = _norm(xf, norm_ffn_w[i], mods, m0 + 4, m0 + 3, BF16)
        act, w_dn = _swiglu_in(h2, w_ffn_in, w_ffn_out, i)
        xf = _matmul_residual(act, w_dn, xf, mods, m0 + 5, "ffn_out")

    out = _norm(xf, final_norm_w, mods, depth * N_MOD, depth * N_MOD, x.dtype)
    return out.reshape(b, s, d)
```
